```python
import math
import jax
import jax.numpy as jnp
from jax import lax
import numpy as np

D_MODEL = 1024
BATCH = 4
SEQ = 4096
DEPTH = 4

HEAD_DIM = 128
BRANCH_WIDTH = 512
N_BRANCHES = 3
DN_HEADS = BRANCH_WIDTH // HEAD_DIM
DN_CHUNK = 64
CONV_WIDTH = 4
SG_GROUPS = 4
SG_GROUP_DIM = BRANCH_WIDTH // SG_GROUPS
SG_CHUNK = 128
FOX_HEADS = BRANCH_WIDTH // HEAD_DIM
FOX_BLOCK = 128
NORM_EPS = 1e-6
LN_EPS = 1e-5
IN_SIZES = (3 * BRANCH_WIDTH, BRANCH_WIDTH, DN_HEADS, DN_HEADS,
            2 * BRANCH_WIDTH, BRANCH_WIDTH,
            3 * BRANCH_WIDTH, BRANCH_WIDTH, FOX_HEADS,
            N_BRANCHES * D_MODEL)
D_IN = 3 * BRANCH_WIDTH + BRANCH_WIDTH + 2 * DN_HEADS + 3 * BRANCH_WIDTH + 4 * BRANCH_WIDTH + FOX_HEADS + N_BRANCHES * D_MODEL

kernel_name = 'hybrid_delta_sgmlp_fox_gated_merge'


def rms_norm(x, w, eps=NORM_EPS):
    xf = x.astype(jnp.float32)
    y = xf * lax.rsqrt(jnp.mean(xf * xf, axis=-1, keepdims=True) + eps)
    return (y * w.astype(jnp.float32)).astype(x.dtype)


def layer_norm(x, w, b, eps=LN_EPS):
    xf = x.astype(jnp.float32)
    mu = jnp.mean(xf, axis=-1, keepdims=True)
    xc = xf - mu
    var = jnp.mean(xc * xc, axis=-1, keepdims=True)
    return (xc * lax.rsqrt(var + eps) * w.astype(jnp.float32) + b.astype(jnp.float32)).astype(x.dtype)


def l2_norm(x, eps=1e-6):
    xf = x.astype(jnp.float32)
    return xf * lax.rsqrt(jnp.sum(xf * xf, axis=-1, keepdims=True) + eps)


def split_columns(z):
    parts, start = [], 0
    for size in IN_SIZES:
        parts.append(z[..., start:start + size])
        start += size
    return parts


def causal_depthwise_conv(x, w):
    K, C = w.shape
    return lax.conv_general_dilated(
        x, w[:, None, :].astype(x.dtype), window_strides=(1,), padding=[(K - 1, 0)],
        dimension_numbers=('NWC', 'WIO', 'NWC'), feature_group_count=C)


def gated_delta_rule_chunked(q, k, v, g, beta):
    Bn, T, H, dk = q.shape
    dv = v.shape[-1]
    C = DN_CHUNK
    N = T // C

    def to_chunks(t):
        t = t.reshape((Bn, N, C, H) + t.shape[3:])
        return jnp.moveaxis(t, 3, 1)

    q = to_chunks(q * dk ** -0.5)
    k = to_chunks(k)
    v = to_chunks(v)
    beta = to_chunks(beta)
    gc = jnp.cumsum(to_chunks(g), axis=-1)
    causal = jnp.tril(jnp.ones((C, C), dtype=bool))
    strict = jnp.tril(jnp.ones((C, C), dtype=bool), -1)
    decay = jnp.exp(jnp.where(causal, gc[..., :, None] - gc[..., None, :], -jnp.inf))
    kb = k * beta[..., None]
    L = jnp.where(strict, jnp.einsum('bhnck,bhnsk->bhncs', kb, k) * decay, 0.0)
    eye = jnp.eye(C, dtype=q.dtype)
    Tm = lax.linalg.triangular_solve(eye + L, jnp.broadcast_to(eye, L.shape),
                                     left_side=True, lower=True)
    u = jnp.einsum('bhncs,bhnsv->bhncv', Tm, v * beta[..., None])
    w = jnp.einsum('bhncs,bhnsk->bhnck', Tm, kb * jnp.exp(gc)[..., None])
    qk = jnp.einsum('bhnck,bhnsk->bhncs', q, k) * decay
    q_dec = q * jnp.exp(gc)[..., None]
    k_dec = k * jnp.exp(gc[..., -1:] - gc)[..., None]
    g_last = jnp.exp(gc[..., -1])

    def step(S, inp):
        u_n, w_n, qk_n, qd_n, kd_n, gl_n = inp
        v_new = u_n - jnp.einsum('bhck,bhkv->bhcv', w_n, S)
        o_n = jnp.einsum('bhck,bhkv->bhcv', qd_n, S) + jnp.einsum('bhcs,bhsv->bhcv', qk_n, v_new)
        S = S * gl_n[..., None, None] + jnp.einsum('bhck,bhcv->bhkv', kd_n, v_new)
        return S, o_n

    xs = tuple(jnp.moveaxis(t, 2, 0) for t in (u, w, qk, q_dec, k_dec, g_last))
    S0 = jnp.zeros((Bn, H, dk, dv), q.dtype)
    _, o = lax.scan(step, S0, xs)
    o = jnp.moveaxis(o, 0, 2)
    return jnp.moveaxis(o, 1, 3).reshape(Bn, T, H, dv)


def gated_deltanet_branch(qkv, z, beta_logit, alpha_logit, conv_w, a_log, dt_bias, norm_w):
    Bn, T, _ = qkv.shape
    qkv = jax.nn.silu(causal_depthwise_conv(qkv, conv_w))
    q, k, v = jnp.split(qkv, 3, axis=-1)
    shp = (Bn, T, DN_HEADS, HEAD_DIM)
    q = l2_norm(q.reshape(shp))
    k = l2_norm(k.reshape(shp))
    v = v.reshape(shp).astype(jnp.float32)
    beta = jax.nn.sigmoid(beta_logit.astype(jnp.float32))
    g = -jnp.exp(a_log.astype(jnp.float32)) * jax.nn.softplus(
        alpha_logit.astype(jnp.float32) + dt_bias.astype(jnp.float32))
    o = rms_norm(gated_delta_rule_chunked(q, k, v, g, beta), norm_w)
    return o.reshape(Bn, T, BRANCH_WIDTH).astype(z.dtype) * jax.nn.silu(z)


def spatial_gating_branch(uv, z, ln_w, ln_b, w_s, b_s):
    Bn, T, _ = uv.shape
    u, v = jnp.split(uv, 2, axis=-1)
    v = layer_norm(v, ln_w, ln_b)
    v = v.reshape(Bn, T // SG_CHUNK, SG_CHUNK, SG_GROUPS, SG_GROUP_DIM)
    w_causal = jnp.where(jnp.tril(jnp.ones((SG_CHUNK, SG_CHUNK), dtype=bool)), w_s, 0.0).astype(v.dtype)
    mixed = jnp.einsum('gts,bnsgc->bntgc', w_causal, v) + b_s.T[:, :, None].astype(v.dtype)
    return u * mixed.reshape(Bn, T, BRANCH_WIDTH) * jax.nn.silu(z)


def forgetting_attention_branch(qkv, z, f_logit, f_bias, qn_w, kn_w):
    Bn, T, _ = qkv.shape
    q, k, v = jnp.split(qkv, 3, axis=-1)
    shp = (Bn, T, FOX_HEADS, HEAD_DIM)
    q = rms_norm(q.reshape(shp), qn_w).astype(jnp.float32) * HEAD_DIM ** -0.5
    k = rms_norm(k.reshape(shp), kn_w).astype(jnp.float32)
    v = v.reshape(shp).astype(jnp.float32)
    log_f = jax.nn.log_sigmoid(f_logit.astype(jnp.float32) + f_bias.astype(jnp.float32))
    c = jnp.cumsum(log_f, axis=1).transpose(0, 2, 1)
    outs = []
    for blk in range(T // FOX_BLOCK):
        start, end = blk * FOX_BLOCK, (blk + 1) * FOX_BLOCK
        s = jnp.einsum('bqhd,bkhd->bhqk', q[:, start:end], k[:, :end])
        s = s + (c[:, :, start:end, None] - c[:, :, None, :end])
        mask = jnp.arange(end)[None, :] <= jnp.arange(start, end)[:, None]
        p = jax.nn.softmax(jnp.where(mask, s, -jnp.inf), axis=-1)
        outs.append(jnp.einsum('bhqk,bkhd->bqhd', p, v[:, :end]))
    o = jnp.concatenate(outs, axis=1).reshape(Bn, T, BRANCH_WIDTH)
    return o.astype(z.dtype) * jax.nn.silu(z)


def setup_inputs(seed: int = 0) -> dict:
    key = jax.random.key(seed)
    ks = jax.random.split(key, 20)
    f32 = jnp.float32
    L, D, W = DEPTH, D_MODEL, BRANCH_WIDTH

    def nrm(k, shape, scale):
        return scale * jax.random.normal(k, shape, f32)

    x = jax.random.normal(ks[0], (BATCH, SEQ, D), f32)
    norm_w = 1.0 + nrm(ks[1], (L, D), 0.02)
    w_in = nrm(ks[2], (L, D, D_IN), D ** -0.5)
    f_bias = jax.random.uniform(ks[3], (L, FOX_HEADS), f32, 1.0, 5.0)
    conv_w = nrm(ks[4], (L, CONV_WIDTH, 3 * W), CONV_WIDTH ** -0.5)
    a_log = jnp.log(jax.random.uniform(ks[5], (L, DN_HEADS), f32, 1.0, 16.0))
    dt = jnp.exp(jax.random.uniform(ks[6], (L, DN_HEADS), f32, math.log(1e-3), math.log(1e-1)))
    dt_bias = dt + jnp.log(-jnp.expm1(-dt))
    dn_norm_w = 1.0 + nrm(ks[7], (L, HEAD_DIM), 0.02)
    sg_ln_w = 1.0 + nrm(ks[8], (L, W), 0.02)
    sg_ln_b = nrm(ks[9], (L, W), 0.02)
    w_spatial = nrm(ks[10], (L, SG_GROUPS, SG_CHUNK, SG_CHUNK), SG_CHUNK ** -0.5)
    b_spatial = 1.0 + nrm(ks[11], (L, SG_GROUPS, SG_CHUNK), 0.1)
    fox_qnorm_w = 1.0 + nrm(ks[12], (L, HEAD_DIM), 0.02)
    fox_knorm_w = 1.0 + nrm(ks[13], (L, HEAD_DIM), 0.02)
    w_branch = nrm(ks[14], (L, N_BRANCHES, W, D), W ** -0.5)
    w_out = nrm(ks[15], (L, D, D), D ** -0.5)
    final_norm_w = 1.0 + nrm(ks[16], (D,), 0.02)
    return {'x': x, 'norm_w': norm_w, 'w_in': w_in, 'f_bias': f_bias, 'conv_w': conv_w,
            'a_log': a_log, 'dt_bias': dt_bias, 'dn_norm_w': dn_norm_w,
            'sg_ln_w': sg_ln_w, 'sg_ln_b': sg_ln_b, 'w_spatial': w_spatial, 'b_spatial': b_spatial,
            'fox_qnorm_w': fox_qnorm_w, 'fox_knorm_w': fox_knorm_w,
            'w_branch': w_branch, 'w_out': w_out, 'final_norm_w': final_norm_w}


def reference(x, norm_w, w_in, f_bias, conv_w, a_log, dt_bias, dn_norm_w,
              sg_ln_w, sg_ln_b, w_spatial, b_spatial, fox_qnorm_w, fox_knorm_w,
              w_branch, w_out, final_norm_w):
    Bn, T, _ = x.shape
    for l in range(DEPTH):
        h = rms_norm(x, norm_w[l])
        proj = jnp.einsum('btd,de->bte', h, w_in[l])
        (dn_qkv, dn_z, dn_beta, dn_alpha, sg_uv, sg_z,
         fox_qkv, fox_z, fox_f, gate_logits) = split_columns(proj)
        o_delta = gated_deltanet_branch(dn_qkv, dn_z, dn_beta, dn_alpha,
                                        conv_w[l], a_log[l], dt_bias[l], dn_norm_w[l])
        o_sg = spatial_gating_branch(sg_uv, sg_z, sg_ln_w[l], sg_ln_b[l], w_spatial[l], b_spatial[l])
        o_fox = forgetting_attention_branch(fox_qkv, fox_z, fox_f, f_bias[l],
                                            fox_qnorm_w[l], fox_knorm_w[l])
        branches = jnp.stack([o_delta, o_sg, o_fox], axis=2)
        up = jnp.einsum('btnw,nwd->btnd', branches, w_branch[l])
        gates = jax.nn.sigmoid(gate_logits.reshape(Bn, T, N_BRANCHES, D_MODEL))
        merged = jnp.sum(gates * up, axis=2)
        x = x + jnp.einsum('btd,de->bte', merged, w_out[l])
    return rms_norm(x, final_norm_w)
```

```python
import functools

import jax
import jax.numpy as jnp
from jax import lax
from jax.experimental import pallas as pl
from jax.experimental.pallas import tpu as pltpu

F32 = jnp.float32
BF16 = jnp.bfloat16

D_MODEL = 1024
HEAD_DIM = 128
N_HEADS = 4
WIDTH = N_HEADS * HEAD_DIM
N_BRANCHES = 3
DN_CHUNK = 64
SUPER = 2 * DN_CHUNK
CONV_WIDTH = 4
SG_CHUNK = 128
NORM_EPS = 1e-6
LN_EPS = 1e-5
L2_EPS = 1e-6
LANES = 128
SUBLANES = 8

GATES_OFF = 0
DN_OFF = GATES_OFF + N_BRANCHES * D_MODEL
SG_OFF = DN_OFF + 4 * WIDTH
FOX_OFF = SG_OFF + 3 * WIDTH
SMALL_OFF = FOX_OFF + 4 * WIDTH
PROJ_TILE_N = 1024
PACKED_COLS = -(-(SMALL_OFF + LANES) // PROJ_TILE_N) * PROJ_TILE_N
BETA_LANE, ALPHA_LANE, FORGET_LANE = 0, N_HEADS, 2 * N_HEADS

VMEM_LIMIT = 48 * 1024 * 1024


def _dot(a, b):
    return jnp.dot(a.astype(BF16), b.astype(BF16), preferred_element_type=F32)


def _dot_nt(a, b):
    return lax.dot_general(a.astype(BF16), b.astype(BF16), (((1,), (1,)), ((), ())),
                           preferred_element_type=F32)


def _dot_exact_lhs(a01, x):
    a = a01.astype(BF16)
    x1 = x.astype(BF16)
    r1 = x - x1.astype(F32)
    x2 = r1.astype(BF16)
    x3 = (r1 - x2.astype(F32)).astype(BF16)
    d = lambda p: jnp.dot(a, p, preferred_element_type=F32)
    return d(x1) + d(x2) + d(x3)


def _sigmoid(x):
    return 1.0 / (1.0 + jnp.exp(-x))


def _silu(x):
    return x * _sigmoid(x)


def _softplus(x):
    return jnp.maximum(x, 0.0) + jnp.log(1.0 + jnp.exp(-jnp.abs(x)))


def _log_sigmoid(x):
    return -_softplus(-x)


def _inproj_body(x_ref, nw_ref, w_ref, o_ref, h_ref):
    @pl.when(pl.program_id(1) == 0)
    def _():
        x = x_ref[...]
        ms = jnp.mean(x * x, axis=-1, keepdims=True)
        h_ref[...] = (x * lax.rsqrt(ms + NORM_EPS) * nw_ref[...]).astype(BF16)

    o_ref[...] = jnp.dot(h_ref[...], w_ref[...], preferred_element_type=F32)


def _inproj(x2d, norm_w, w_packed, tm):
    n = x2d.shape[0]
    return pl.pallas_call(
        _inproj_body,
        grid=(n // tm, PACKED_COLS // PROJ_TILE_N),
        in_specs=[
            pl.BlockSpec((tm, D_MODEL), lambda i, j: (i, 0)),
            pl.BlockSpec((1, D_MODEL), lambda i, j: (0, 0)),
            pl.BlockSpec((D_MODEL, PROJ_TILE_N), lambda i, j: (0, j)),
        ],
        out_specs=pl.BlockSpec((tm, PROJ_TILE_N), lambda i, j: (i, j)),
        out_shape=jax.ShapeDtypeStruct((n, PACKED_COLS), F32),
        scratch_shapes=[pltpu.VMEM((tm, D_MODEL), BF16)],
        compiler_params=pltpu.CompilerParams(
            dimension_semantics=("parallel", "arbitrary"), vmem_limit_bytes=VMEM_LIMIT),
        name="inproj",
    )(x2d, norm_w.reshape(1, D_MODEL), w_packed)


def _unit_lower_inverse(low, eye, m8, moffs):
    l8 = jnp.where(m8, low, 0.0)
    l8_2 = _dot(l8, l8)
    l8_3 = _dot(l8, l8_2)
    l8_4 = _dot(l8_2, l8_2)
    p1 = eye - l8 + l8_2 - l8_3
    x = p1 + _dot(p1, l8_4)
    for moff in moffs:
        loff = jnp.where(moff, low, 0.0)
        x = x - _dot(_dot(x, loff), x)
    return x


def _delta_body(qkv_ref, z_ref, small_ref, convw_ref, alog_ref, dtb_ref, nw_ref, o_ref,
                ext_ref, y_ref, bt_ref, gt_ref, s_ref):
    t = pl.program_id(1)
    tm = qkv_ref.shape[0]
    halo = SUBLANES

    @pl.when(t == 0)
    def _():
        ext_ref[0:halo, :] = jnp.zeros((halo, 3 * WIDTH), F32)
        s_ref[...] = jnp.zeros(s_ref.shape, F32)

    @pl.when(t > 0)
    def _():
        ext_ref[0:halo, :] = ext_ref[tm:tm + halo, :]

    ext_ref[halo:halo + tm, :] = qkv_ref[...]
    acc = None
    for j in range(CONV_WIDTH):
        off = halo - (CONV_WIDTH - 1) + j
        term = convw_ref[j:j + 1, :] * ext_ref[off:off + tm, :]
        acc = term if acc is None else acc + term
    y_ref[...] = _silu(acc)

    sm = small_ref[...]
    bt_ref[...] = _sigmoid(sm)
    gt_ref[...] = -jnp.exp(alog_ref[...]) * _softplus(sm + dtb_ref[...])

    row = lax.broadcasted_iota(jnp.int32, (SUPER, SUPER), 0)
    col = lax.broadcasted_iota(jnp.int32, (SUPER, SUPER), 1)
    blk = lambda idx, size: jnp.right_shift(idx, size.bit_length() - 1)
    same_chunk = blk(row, DN_CHUNK) == blk(col, DN_CHUNK)
    causal = same_chunk & (col <= row)
    strict = same_chunk & (col < row)
    causal_ones = jnp.where(causal, 1.0, 0.0).astype(BF16)
    eye = jnp.where(row == col, 1.0, 0.0).astype(F32)
    m8 = blk(row, 8) == blk(col, 8)
    moffs = [(blk(row, 2 * s) == blk(col, 2 * s)) & (blk(row, s) != blk(col, s)) for s in (8, 16, 32)]
    first_rows = row < DN_CHUNK
    first_cols = col < DN_CHUNK
    nw = nw_ref[...]

    def super_chunk(sc, carry):
        r0 = pl.multiple_of(sc * SUPER, SUPER)
        rows = pl.ds(r0, SUPER)
        g = gt_ref[rows, :]
        gc = _dot_exact_lhs(causal_ones, g)
        gct = gc.T
        bt = bt_ref[rows, :]
        egc = jnp.exp(gc)
        gl0 = gc[DN_CHUNK - 1:DN_CHUNK, :]
        gl1 = gc[SUPER - 1:SUPER, :]
        glast = jnp.where(first_rows, gl0, gl1)
        ekd = jnp.exp(glast - gc)
        egl0 = jnp.exp(gl0)
        egl1 = jnp.exp(gl1)
        for h in range(N_HEADS):
            hc = slice(h * HEAD_DIM, (h + 1) * HEAD_DIM)
            q = y_ref[rows, h * HEAD_DIM:(h + 1) * HEAD_DIM]
            k = y_ref[rows, WIDTH + h * HEAD_DIM:WIDTH + (h + 1) * HEAD_DIM]
            v = y_ref[rows, 2 * WIDTH + h * HEAD_DIM:2 * WIDTH + (h + 1) * HEAD_DIM]
            q = q * lax.rsqrt(jnp.sum(q * q, axis=-1, keepdims=True) + L2_EPS) * (HEAD_DIM ** -0.5)
            k = k * lax.rsqrt(jnp.sum(k * k, axis=-1, keepdims=True) + L2_EPS)
            beta = bt[:, BETA_LANE + h:BETA_LANE + h + 1]
            gcc = gc[:, ALPHA_LANE + h:ALPHA_LANE + h + 1]
            gcr = gct[ALPHA_LANE + h:ALPHA_LANE + h + 1, :]
            dec = jnp.exp(jnp.where(causal, gcc - gcr, -jnp.inf))
            kb = k * beta
            low = jnp.where(strict, _dot_nt(kb, k) * dec, 0.0)
            tmat = _unit_lower_inverse(low, eye, m8, moffs)
            u = _dot(tmat, v * beta)
            w = _dot(tmat, kb * egc[:, ALPHA_LANE + h:ALPHA_LANE + h + 1])
            qk = _dot_nt(q, k) * dec
            qd = q * egc[:, ALPHA_LANE + h:ALPHA_LANE + h + 1]
            kdt = (k * ekd[:, ALPHA_LANE + h:ALPHA_LANE + h + 1]).T
            kdt0 = jnp.where(first_cols, kdt, 0.0)
            kdt1 = jnp.where(first_cols, 0.0, kdt)
            s0 = s_ref[h]
            vn0 = u - _dot(w, s0)
            s1 = s0 * egl0[:, ALPHA_LANE + h:ALPHA_LANE + h + 1] + _dot(kdt0, vn0)
            vn1 = u - _dot(w, s1)
            s2 = s1 * egl1[:, ALPHA_LANE + h:ALPHA_LANE + h + 1] + _dot(kdt1, vn1)
            vn = jnp.where(first_rows, vn0, vn1)
            o = jnp.where(first_rows, _dot(qd, s0), _dot(qd, s1)) + _dot(qk, vn)
            s_ref[h] = s2
            o = o * lax.rsqrt(jnp.mean(o * o, axis=-1, keepdims=True) + NORM_EPS) * nw
            o_ref[rows, hc] = (o * _silu(z_ref[rows, hc])).astype(o_ref.dtype)
        return carry

    lax.fori_loop(0, tm // SUPER, super_chunk, 0)


def _delta_branch(proj, conv_w, a_log, dt_bias, norm_w, batch, seq, tm):
    n = proj.shape[0]
    nt = seq // tm
    lane_row = lambda vals, off: jnp.zeros((1, LANES), F32).at[0, off:off + N_HEADS].set(vals)
    tok = lambda width, col: pl.BlockSpec((tm, width), lambda b, t: (b * nt + t, col))
    full = lambda shape: pl.BlockSpec(shape, lambda b, t: (0,) * len(shape))
    return pl.pallas_call(
        _delta_body,
        grid=(batch, nt),
        in_specs=[
            tok(3 * WIDTH, DN_OFF // (3 * WIDTH)),
            tok(WIDTH, (DN_OFF + 3 * WIDTH) // WIDTH),
            tok(LANES, SMALL_OFF // LANES),
            full((CONV_WIDTH, 3 * WIDTH)),
            full((1, LANES)),
            full((1, LANES)),
            full((1, HEAD_DIM)),
        ],
        out_specs=pl.BlockSpec((tm, WIDTH), lambda b, t: (b * nt + t, 0)),
        out_shape=jax.ShapeDtypeStruct((n, WIDTH), BF16),
        scratch_shapes=[
            pltpu.VMEM((tm + 2 * SUBLANES, 3 * WIDTH), F32),
            pltpu.VMEM((tm, 3 * WIDTH), F32),
            pltpu.VMEM((tm, LANES), F32),
            pltpu.VMEM((tm, LANES), F32),
            pltpu.VMEM((N_HEADS, HEAD_DIM, HEAD_DIM), F32),
        ],
        compiler_params=pltpu.CompilerParams(
            dimension_semantics=("parallel", "arbitrary"), vmem_limit_bytes=VMEM_LIMIT),
        name="delta",
    )(proj, proj, proj, conv_w, lane_row(a_log, ALPHA_LANE), lane_row(dt_bias, ALPHA_LANE),
      norm_w.reshape(1, HEAD_DIM))


def _sg_body(u_ref, v_ref, z_ref, lnw_ref, lnb_ref, ws_ref, bs_ref, o_ref):
    tm = u_ref.shape[0]
    v = v_ref[...]
    mu = jnp.mean(v, axis=-1, keepdims=True)
    vc = v - mu
    var = jnp.mean(vc * vc, axis=-1, keepdims=True)
    vn = (vc * lax.rsqrt(var + LN_EPS) * lnw_ref[...] + lnb_ref[...]).astype(BF16)
    row = lax.broadcasted_iota(jnp.int32, (SG_CHUNK, SG_CHUNK), 0)
    col = lax.broadcasted_iota(jnp.int32, (SG_CHUNK, SG_CHUNK), 1)
    tril = col <= row
    gate = u_ref[...] * _silu(z_ref[...])
    bs = bs_ref[...]
    for g in range(N_HEADS):
        wg = jnp.where(tril, ws_ref[g], 0.0).astype(BF16)
        bg = bs[:, g:g + 1]
        for c in range(tm // SG_CHUNK):
            rs = slice(c * SG_CHUNK, (c + 1) * SG_CHUNK)
            cs = slice(g * HEAD_DIM, (g + 1) * HEAD_DIM)
            mixed = jnp.dot(wg, vn[rs, cs], preferred_element_type=F32) + bg
            o_ref[rs, cs] = (gate[rs, cs] * mixed).astype(o_ref.dtype)


def _sg_branch(proj, ln_w, ln_b, w_s, b_s, tm):
    n = proj.shape[0]
    tok = lambda col: pl.BlockSpec((tm, WIDTH), lambda i: (i, col))
    full = lambda shape: pl.BlockSpec(shape, lambda i: (0,) * len(shape))
    bs_t = jnp.zeros((SG_CHUNK, LANES), F32).at[:, :N_HEADS].set(b_s.T)
    return pl.pallas_call(
        _sg_body,
        grid=(n // tm,),
        in_specs=[
            tok(SG_OFF // WIDTH), tok(SG_OFF // WIDTH + 1), tok(SG_OFF // WIDTH + 2),
            full((1, WIDTH)), full((1, WIDTH)),
            full((N_HEADS, SG_CHUNK, SG_CHUNK)), full((SG_CHUNK, LANES)),
        ],
        out_specs=pl.BlockSpec((tm, WIDTH), lambda i: (i, 0)),
        out_shape=jax.ShapeDtypeStruct((n, WIDTH), BF16),
        compiler_params=pltpu.CompilerParams(
            dimension_semantics=("parallel",), vmem_limit_bytes=VMEM_LIMIT),
        name="spatial_gate",
    )(proj, proj, proj, ln_w.reshape(1, WIDTH), ln_b.reshape(1, WIDTH), w_s, bs_t)


def _fox_prep_body(q_ref, k_ref, v_ref, small_ref, fb_ref, qw_ref, kw_ref,
                   qn_ref, kn_ref, vb_ref, ccol_ref, crow_ref, carry_ref):
    t = pl.program_id(1)
    tm = q_ref.shape[0]

    @pl.when(t == 0)
    def _():
        carry_ref[...] = jnp.zeros(carry_ref.shape, F32)

    for h in range(N_HEADS):
        cs = slice(h * HEAD_DIM, (h + 1) * HEAD_DIM)
        q = q_ref[:, cs]
        k = k_ref[:, cs]
        q = q * lax.rsqrt(jnp.mean(q * q, axis=-1, keepdims=True) + NORM_EPS) * qw_ref[...]
        k = k * lax.rsqrt(jnp.mean(k * k, axis=-1, keepdims=True) + NORM_EPS) * kw_ref[...]
        qn_ref[:, cs] = (q * (HEAD_DIM ** -0.5)).astype(BF16)
        kn_ref[:, cs] = k.astype(BF16)
    vb_ref[...] = v_ref[...].astype(BF16)

    logf = _log_sigmoid(small_ref[...] + fb_ref[...])
    row = lax.broadcasted_iota(jnp.int32, (tm, tm), 0)
    col = lax.broadcasted_iota(jnp.int32, (tm, tm), 1)
    c = _dot_exact_lhs(jnp.where(col <= row, 1.0, 0.0), logf) + carry_ref[0:1, :]
    carry_ref[0:1, :] = c[tm - 1:tm, :]
    ct = c.T
    for h in range(N_HEADS):
        lane = FORGET_LANE + h
        ccol_ref[:, h * HEAD_DIM:(h + 1) * HEAD_DIM] = jnp.broadcast_to(c[:, lane:lane + 1], (tm, HEAD_DIM))
        crow_ref[h * SUBLANES:(h + 1) * SUBLANES, :] = jnp.broadcast_to(ct[lane:lane + 1, :], (SUBLANES, tm))


def _fox_prep(proj, f_bias, qn_w, kn_w, batch, seq, tm):
    n = proj.shape[0]
    nt = seq // tm
    tok = lambda width, col: pl.BlockSpec((tm, width), lambda b, t: (b * nt + t, col))
    full = lambda shape: pl.BlockSpec(shape, lambda b, t: (0,) * len(shape))
    fb = jnp.zeros((1, LANES), F32).at[0, FORGET_LANE:FORGET_LANE + N_HEADS].set(f_bias)
    out_tok = pl.BlockSpec((tm, WIDTH), lambda b, t: (b * nt + t, 0))
    return pl.pallas_call(
        _fox_prep_body,
        grid=(batch, nt),
        in_specs=[
            tok(WIDTH, FOX_OFF // WIDTH), tok(WIDTH, FOX_OFF // WIDTH + 1), tok(WIDTH, FOX_OFF // WIDTH + 2),
            tok(LANES, SMALL_OFF // LANES),
            full((1, LANES)), full((1, HEAD_DIM)), full((1, HEAD_DIM)),
        ],
        out_specs=[out_tok, out_tok, out_tok, out_tok,
                   pl.BlockSpec((N_HEADS * SUBLANES, tm), lambda b, t: (b, t))],
        out_shape=[jax.ShapeDtypeStruct((n, WIDTH), BF16)] * 3
        + [jax.ShapeDtypeStruct((n, WIDTH), F32),
           jax.ShapeDtypeStruct((batch * N_HEADS * SUBLANES, seq), F32)],
        scratch_shapes=[pltpu.VMEM((SUBLANES, LANES), F32)],
        compiler_params=pltpu.CompilerParams(
            dimension_semantics=("parallel", "arbitrary"), vmem_limit_bytes=VMEM_LIMIT),
        name="fox_prep",
    )(proj, proj, proj, proj, fb, qn_w.reshape(1, HEAD_DIM), kn_w.reshape(1, HEAD_DIM))


def _fox_attn_body(q_ref, k_ref, v_ref, cq_ref, ck_ref, z_ref, o_ref, m_ref, l_ref, acc_ref):
    i = pl.program_id(2)
    j = pl.program_id(3)
    bq = q_ref.shape[0]
    bk = k_ref.shape[0]

    @pl.when(j == 0)
    def _():
        m_ref[...] = jnp.full(m_ref.shape, -jnp.inf, F32)
        l_ref[...] = jnp.zeros(l_ref.shape, F32)
        acc_ref[...] = jnp.zeros(acc_ref.shape, F32)

    @pl.when(j <= i)
    def _():
        s = lax.dot_general(q_ref[...], k_ref[...], (((1,), (1,)), ((), ())),
                            preferred_element_type=F32)
        s = s + (cq_ref[:, 0:1] - ck_ref[0:1, :])
        qpos = i * bq + lax.broadcasted_iota(jnp.int32, (bq, bk), 0)
        kpos = j * bk + lax.broadcasted_iota(jnp.int32, (bq, bk), 1)
        s = jnp.where(kpos <= qpos, s, -jnp.inf)
        m_prev = m_ref[...]
        m_new = jnp.maximum(m_prev, jnp.max(s, axis=-1, keepdims=True))
        alpha = jnp.exp(m_prev - m_new)
        p = jnp.exp(s - m_new)
        l_ref[...] = alpha * l_ref[...] + jnp.sum(p, axis=-1, keepdims=True)
        acc_ref[...] = alpha * acc_ref[...] + jnp.dot(p.astype(BF16), v_ref[...],
                                                      preferred_element_type=F32)
        m_ref[...] = m_new

    @pl.when(j == i)
    def _():
        o = acc_ref[...] / l_ref[...]
        o_ref[...] = (o * _silu(z_ref[...])).astype(o_ref.dtype)


def _fox_attention(qn, kn, vb, ccol, crow, proj, batch, seq, blk):
    n = qn.shape[0]
    nb = seq // blk
    qspec = lambda arr_col: pl.BlockSpec((blk, HEAD_DIM), lambda b, h, i, j: (b * nb + i, arr_col(h)))
    kspec = pl.BlockSpec((blk, HEAD_DIM), lambda b, h, i, j: (b * nb + jnp.minimum(j, i), h))
    return pl.pallas_call(
        _fox_attn_body,
        grid=(batch, N_HEADS, nb, nb),
        in_specs=[
            qspec(lambda h: h), kspec, kspec,
            qspec(lambda h: h),
            pl.BlockSpec((SUBLANES, blk), lambda b, h, i, j: (b * N_HEADS + h, jnp.minimum(j, i))),
            qspec(lambda h: (FOX_OFF + 3 * WIDTH) // HEAD_DIM + h),
        ],
        out_specs=qspec(lambda h: h),
        out_shape=jax.ShapeDtypeStruct((n, WIDTH), BF16),
        scratch_shapes=[pltpu.VMEM((blk, 1), F32), pltpu.VMEM((blk, 1), F32),
                        pltpu.VMEM((blk, HEAD_DIM), F32)],
        compiler_params=pltpu.CompilerParams(
            dimension_semantics=("parallel", "parallel", "parallel", "arbitrary"),
            vmem_limit_bytes=VMEM_LIMIT),
        name="fox_attention",
    )(qn, kn, vb, ccol, crow, proj)


def _merge_body(od_ref, os_ref, of_ref, g_ref, x_ref, wb_ref, wo_ref, o_ref):
    merged = None
    for nbr, br_ref in enumerate((od_ref, os_ref, of_ref)):
        up = jnp.dot(br_ref[...], wb_ref[nbr], preferred_element_type=F32)
        term = _sigmoid(g_ref[:, nbr * D_MODEL:(nbr + 1) * D_MODEL]) * up
        merged = term if merged is None else merged + term
    o_ref[...] = x_ref[...] + jnp.dot(merged.astype(BF16), wo_ref[...], preferred_element_type=F32)


def _merge(o_delta, o_sg, o_fox, proj, x2d, w_branch, w_out, tm):
    n = x2d.shape[0]
    tok = lambda width, col: pl.BlockSpec((tm, width), lambda i: (i, col))
    full = lambda shape: pl.BlockSpec(shape, lambda i: (0,) * len(shape))
    return pl.pallas_call(
        _merge_body,
        grid=(n // tm,),
        in_specs=[
            tok(WIDTH, 0), tok(WIDTH, 0), tok(WIDTH, 0),
            tok(N_BRANCHES * D_MODEL, GATES_OFF // (N_BRANCHES * D_MODEL)),
            tok(D_MODEL, 0),
            full((N_BRANCHES, WIDTH, D_MODEL)), full((D_MODEL, D_MODEL)),
        ],
        out_specs=tok(D_MODEL, 0),
        out_shape=jax.ShapeDtypeStruct((n, D_MODEL), F32),
        compiler_params=pltpu.CompilerParams(
            dimension_semantics=("parallel",), vmem_limit_bytes=VMEM_LIMIT),
        name="merge",
    )(o_delta, o_sg, o_fox, proj, x2d, w_branch, w_out)


def _final_norm_body(x_ref, w_ref, o_ref):
    x = x_ref[...]
    ms = jnp.mean(x * x, axis=-1, keepdims=True)
    o_ref[...] = x * lax.rsqrt(ms + NORM_EPS) * w_ref[...]


def _final_norm(x2d, w, tm):
    n = x2d.shape[0]
    return pl.pallas_call(
        _final_norm_body,
        grid=(n // tm,),
        in_specs=[pl.BlockSpec((tm, D_MODEL), lambda i: (i, 0)),
                  pl.BlockSpec((1, D_MODEL), lambda i: (0, 0))],
        out_specs=pl.BlockSpec((tm, D_MODEL), lambda i: (i, 0)),
        out_shape=jax.ShapeDtypeStruct((n, D_MODEL), F32),
        compiler_params=pltpu.CompilerParams(dimension_semantics=("parallel",)),
        name="final_norm",
    )(x2d, w.reshape(1, D_MODEL))


def _pack_w_in(w):
    sizes = (3 * WIDTH, WIDTH, N_HEADS, N_HEADS, 2 * WIDTH, WIDTH, 3 * WIDTH, WIDTH, N_HEADS,
             N_BRANCHES * D_MODEL)
    starts = [0]
    for s in sizes:
        starts.append(starts[-1] + s)
    part = lambda a, b: w[:, starts[a]:starts[b]]
    pieces = [part(9, 10), part(0, 2), part(4, 6), part(6, 8), part(2, 4), part(8, 9)]
    used = sum(p.shape[1] for p in pieces)
    pieces.append(jnp.zeros((w.shape[0], PACKED_COLS - used), w.dtype))
    return jnp.concatenate(pieces, axis=1).astype(BF16)


def _tile(total, want):
    t = min(total, want)
    assert total % t == 0, (total, t)
    return t


def kernel(x, norm_w, w_in, f_bias, conv_w, a_log, dt_bias, dn_norm_w, sg_ln_w, sg_ln_b,
           w_spatial, b_spatial, fox_qnorm_w, fox_knorm_w, w_branch, w_out, final_norm_w):
    batch, seq, d_model = x.shape
    assert d_model == D_MODEL and seq % SUPER == 0
    n = batch * seq
    depth = w_in.shape[0]
    x2d = x.reshape(n, D_MODEL)
    tm_proj = _tile(n, 1024)
    tm_delta = _tile(seq, 256)
    tm_sg = _tile(n, 512)
    tm_prep = _tile(seq, 512)
    blk_attn = _tile(seq, 512)
    tm_merge = _tile(n, 512)
    for l in range(depth):
        proj = _inproj(x2d, norm_w[l], _pack_w_in(w_in[l]), tm_proj)
        o_delta = _delta_branch(proj, conv_w[l], a_log[l], dt_bias[l], dn_norm_w[l], batch, seq, tm_delta)
        o_sg = _sg_branch(proj, sg_ln_w[l], sg_ln_b[l], w_spatial[l], b_spatial[l], tm_sg)
        qn, kn, vb, ccol, crow = _fox_prep(proj, f_bias[l], fox_qnorm_w[l], fox_knorm_w[l],
                                           batch, seq, tm_prep)
        o_fox = _fox_attention(qn, kn, vb, ccol, crow, proj, batch, seq, blk_attn)
        x2d = _merge(o_delta, o_sg, o_fox, proj, x2d, w_branch[l].astype(BF16), w_out[l].astype(BF16),
                     tm_merge)
    return _final_norm(x2d, final_norm_w, _tile(n, 1024)).reshape(batch, seq, D_MODEL)
```

```python
import functools

import jax
import jax.numpy as jnp
from jax import lax
from jax.experimental import pallas as pl
from jax.experimental.pallas import tpu as pltpu

F32 = jnp.float32
BF16 = jnp.bfloat16

D_MODEL = 1024
HEAD_DIM = 128
N_HEADS = 4
WIDTH = N_HEADS * HEAD_DIM
N_BRANCHES = 3
DN_CHUNK = 64
SUPER = 2 * DN_CHUNK
CONV_WIDTH = 4
SG_CHUNK = 128
NORM_EPS = 1e-6
LN_EPS = 1e-5
L2_EPS = 1e-6
LANES = 128
SUBLANES = 8

GATES_OFF = 0
DN_OFF = GATES_OFF + N_BRANCHES * D_MODEL
SG_OFF = DN_OFF + 4 * WIDTH
FOX_OFF = SG_OFF + 3 * WIDTH
SMALL_OFF = FOX_OFF + 4 * WIDTH
PROJ_TILE_N = 1024
PACKED_COLS = -(-(SMALL_OFF + LANES) // PROJ_TILE_N) * PROJ_TILE_N
BETA_LANE, ALPHA_LANE, FORGET_LANE = 0, N_HEADS, 2 * N_HEADS

VMEM_LIMIT = 48 * 1024 * 1024


def _dot(a, b):
    return jnp.dot(a.astype(BF16), b.astype(BF16), preferred_element_type=F32)


def _dot_nt(a, b):
    return lax.dot_general(a.astype(BF16), b.astype(BF16), (((1,), (1,)), ((), ())),
                           preferred_element_type=F32)


def _dot_exact_lhs(a01, x):
    a = a01.astype(BF16)
    x1 = x.astype(BF16)
    r1 = x - x1.astype(F32)
    x2 = r1.astype(BF16)
    x3 = (r1 - x2.astype(F32)).astype(BF16)
    d = lambda p: jnp.dot(a, p, preferred_element_type=F32)
    return d(x1) + d(x2) + d(x3)


def _sigmoid(x):
    return 1.0 / (1.0 + jnp.exp(-x))


def _silu(x):
    return x * _sigmoid(x)


def _softplus(x):
    return jnp.maximum(x, 0.0) + jnp.log(1.0 + jnp.exp(-jnp.abs(x)))


def _log_sigmoid(x):
    return -_softplus(-x)


def _inproj_body(x_ref, nw_ref, w_ref, o_ref, h_ref):
    @pl.when(pl.program_id(1) == 0)
    def _():
        x = x_ref[...]
        ms = jnp.mean(x * x, axis=-1, keepdims=True)
        h_ref[...] = (x * lax.rsqrt(ms + NORM_EPS) * nw_ref[...]).astype(BF16)

    o_ref[...] = jnp.dot(h_ref[...], w_ref[...], preferred_element_type=F32)


def _inproj(x2d, norm_w, w_packed, layer, tm):
    n = x2d.shape[0]
    return pl.pallas_call(
        _inproj_body,
        grid=(n // tm, PACKED_COLS // PROJ_TILE_N),
        in_specs=[
            pl.BlockSpec((tm, D_MODEL), lambda i, j: (i, 0)),
            pl.BlockSpec((1, D_MODEL), lambda i, j: (0, 0)),
            pl.BlockSpec((None, D_MODEL, PROJ_TILE_N), lambda i, j: (layer, 0, j)),
        ],
        out_specs=pl.BlockSpec((tm, PROJ_TILE_N), lambda i, j: (i, j)),
        out_shape=jax.ShapeDtypeStruct((n, PACKED_COLS), F32),
        scratch_shapes=[pltpu.VMEM((tm, D_MODEL), BF16)],
        compiler_params=pltpu.CompilerParams(
            dimension_semantics=("parallel", "arbitrary"), vmem_limit_bytes=VMEM_LIMIT),
        name="inproj",
    )(x2d, norm_w.reshape(1, D_MODEL), w_packed)


def _unit_lower_inverse(lows, eye, m8, moffs):
    each = lambda f, *lists: [f(*args) for args in zip(*lists)]
    l8 = each(lambda low: jnp.where(m8, low, 0.0), lows)
    l8_2 = each(_dot, l8, l8)
    l8_3 = each(_dot, l8, l8_2)
    l8_4 = each(_dot, l8_2, l8_2)
    p1 = each(lambda a, b, c: eye - a + b - c, l8, l8_2, l8_3)
    x = each(lambda p, l4: p + _dot(p, l4), p1, l8_4)
    for moff in moffs:
        xl = each(lambda xi, low: _dot(xi, jnp.where(moff, low, 0.0)), x, lows)
        x = each(lambda xi, xli: xi - _dot(xli, xi), x, xl)
    return x


def _delta_body(qkv_ref, z_ref, small_ref, convw_ref, alog_ref, dtb_ref, nw_ref, o_ref,
                ext_ref, y_ref, bt_ref, gt_ref, s_ref, u_ref, w_ref, qd_ref, qk_ref, kdt_ref, gl_ref):
    t = pl.program_id(1)
    tm = qkv_ref.shape[0]
    halo = SUBLANES

    @pl.when(t == 0)
    def _():
        ext_ref[0:halo, :] = jnp.zeros((halo, 3 * WIDTH), F32)
        s_ref[...] = jnp.zeros(s_ref.shape, F32)

    @pl.when(t > 0)
    def _():
        ext_ref[0:halo, :] = ext_ref[tm:tm + halo, :]

    ext_ref[halo:halo + tm, :] = qkv_ref[...]
    acc = None
    for j in range(CONV_WIDTH):
        off = halo - (CONV_WIDTH - 1) + j
        term = convw_ref[j:j + 1, :] * ext_ref[off:off + tm, :]
        acc = term if acc is None else acc + term
    y_ref[...] = _silu(acc)

    sm = small_ref[...]
    bt_ref[...] = _sigmoid(sm)
    gt_ref[...] = -jnp.exp(alog_ref[...]) * _softplus(sm + dtb_ref[...])

    row = lax.broadcasted_iota(jnp.int32, (SUPER, SUPER), 0)
    col = lax.broadcasted_iota(jnp.int32, (SUPER, SUPER), 1)
    blk = lambda idx, size: jnp.right_shift(idx, size.bit_length() - 1)
    same_chunk = blk(row, DN_CHUNK) == blk(col, DN_CHUNK)
    causal = same_chunk & (col <= row)
    strict = same_chunk & (col < row)
    causal_ones = jnp.where(causal, 1.0, 0.0).astype(BF16)
    eye = jnp.where(row == col, 1.0, 0.0).astype(F32)
    m8 = blk(row, 8) == blk(col, 8)
    moffs = [(blk(row, 2 * s) == blk(col, 2 * s)) & (blk(row, s) != blk(col, s)) for s in (8, 16, 32)]
    first_rows = row < DN_CHUNK
    first_cols = col < DN_CHUNK
    nw = nw_ref[...]

    each = lambda f, *lists: [f(*args) for args in zip(*lists)]
    units = [(sc, h) for sc in range(tm // SUPER) for h in range(N_HEADS)]
    hcols = lambda h: slice(h * HEAD_DIM, (h + 1) * HEAD_DIM)
    lows, kbs, ks, qs, decs, egcs = [], [], [], [], [], []
    for sc in range(tm // SUPER):
        rows = slice(sc * SUPER, (sc + 1) * SUPER)
        gc = _dot_exact_lhs(causal_ones, gt_ref[rows, :])
        gct = gc.T
        bt = bt_ref[rows, :]
        egc = jnp.exp(gc)
        gl0 = gc[DN_CHUNK - 1:DN_CHUNK, :]
        gl1 = gc[SUPER - 1:SUPER, :]
        ekd = jnp.exp(jnp.where(first_rows, gl0, gl1) - gc)
        egl = (jnp.exp(gl0), jnp.exp(gl1))
        for h in range(N_HEADS):
            lane = slice(ALPHA_LANE + h, ALPHA_LANE + h + 1)
            q = y_ref[rows, h * HEAD_DIM:(h + 1) * HEAD_DIM]
            k = y_ref[rows, WIDTH + h * HEAD_DIM:WIDTH + (h + 1) * HEAD_DIM]
            q = q * lax.rsqrt(jnp.sum(q * q, axis=-1, keepdims=True) + L2_EPS) * (HEAD_DIM ** -0.5)
            k = k * lax.rsqrt(jnp.sum(k * k, axis=-1, keepdims=True) + L2_EPS)
            dec = jnp.exp(jnp.where(causal, gc[:, lane] - gct[lane, :], -jnp.inf))
            kb = k * bt[:, BETA_LANE + h:BETA_LANE + h + 1]
            qd_ref[rows, hcols(h)] = (q * egc[:, lane]).astype(BF16)
            kdt_ref[sc * N_HEADS + h] = (k * ekd[:, lane]).T.astype(BF16)
            for c in range(2):
                gl_ref[(sc * N_HEADS + h) * 2 + c] = jnp.broadcast_to(egl[c][:, lane], (SUBLANES, LANES))
            qs.append(q), ks.append(k), kbs.append(kb), decs.append(dec), egcs.append(egc[:, lane])
    lows = each(lambda kb, k, dec: jnp.where(strict, _dot_nt(kb, k) * dec, 0.0), kbs, ks, decs)
    qks = each(lambda q, k, dec: (_dot_nt(q, k) * dec).astype(BF16), qs, ks, decs)
    tmats = _unit_lower_inverse(lows, eye, m8, moffs)
    for (sc, h), tmat, kb, egc_col, qk in zip(units, tmats, kbs, egcs, qks):
        rows = slice(sc * SUPER, (sc + 1) * SUPER)
        v = y_ref[rows, 2 * WIDTH + h * HEAD_DIM:2 * WIDTH + (h + 1) * HEAD_DIM]
        u_ref[rows, hcols(h)] = _dot(tmat, v * bt_ref[rows, BETA_LANE + h:BETA_LANE + h + 1])
        w_ref[rows, hcols(h)] = _dot(tmat, kb * egc_col).astype(BF16)
        qk_ref[sc * N_HEADS + h] = qk

    heads = list(range(N_HEADS))

    def scan_step(sc, carry):
        rows = pl.ds(pl.multiple_of(sc * SUPER, SUPER), SUPER)
        unit = lambda h: sc * N_HEADS + h
        u = [u_ref[rows, hcols(h)] for h in heads]
        w = [w_ref[rows, hcols(h)] for h in heads]
        kdt = [kdt_ref[unit(h)] for h in heads]
        zero = jnp.zeros((HEAD_DIM, SUPER), BF16)
        kdt0 = [jnp.where(first_cols, x, zero) for x in kdt]
        kdt1 = [jnp.where(first_cols, zero, x) for x in kdt]
        g0 = [jnp.tile(gl_ref[unit(h) * 2], (HEAD_DIM // SUBLANES, 1)) for h in heads]
        g1 = [jnp.tile(gl_ref[unit(h) * 2 + 1], (HEAD_DIM // SUBLANES, 1)) for h in heads]
        s0 = [s_ref[h] for h in heads]
        vn0 = each(lambda ui, wi, si: ui - _dot(wi, si), u, w, s0)
        s1 = each(lambda si, gi, ki, vi: si * gi + _dot(ki, vi), s0, g0, kdt0, vn0)
        vn1 = each(lambda ui, wi, si: ui - _dot(wi, si), u, w, s1)
        s2 = each(lambda si, gi, ki, vi: si * gi + _dot(ki, vi), s1, g1, kdt1, vn1)
        for h in heads:
            s_ref[h] = s2[h]
            qd = qd_ref[rows, hcols(h)]
            vn = jnp.where(first_rows, vn0[h], vn1[h])
            o = jnp.where(first_rows, _dot(qd, s0[h]), _dot(qd, s1[h])) + _dot(qk_ref[unit(h)], vn)
            o = o * lax.rsqrt(jnp.mean(o * o, axis=-1, keepdims=True) + NORM_EPS) * nw
            o_ref[rows, hcols(h)] = (o * _silu(z_ref[rows, hcols(h)])).astype(o_ref.dtype)
        return carry

    lax.fori_loop(0, tm // SUPER, scan_step, 0)


def _delta_branch(proj, conv_w, a_log, dt_bias, norm_w, batch, seq, tm):
    n = proj.shape[0]
    nt = seq // tm
    units = (tm // SUPER) * N_HEADS
    lane_row = lambda vals, off: jnp.zeros((1, LANES), F32).at[0, off:off + N_HEADS].set(vals)
    tok = lambda width, col: pl.BlockSpec((tm, width), lambda b, t: (b * nt + t, col))
    full = lambda shape: pl.BlockSpec(shape, lambda b, t: (0,) * len(shape))
    return pl.pallas_call(
        _delta_body,
        grid=(batch, nt),
        in_specs=[
            tok(3 * WIDTH, DN_OFF // (3 * WIDTH)),
            tok(WIDTH, (DN_OFF + 3 * WIDTH) // WIDTH),
            tok(LANES, SMALL_OFF // LANES),
            full((CONV_WIDTH, 3 * WIDTH)),
            full((1, LANES)),
            full((1, LANES)),
            full((1, HEAD_DIM)),
        ],
        out_specs=pl.BlockSpec((tm, WIDTH), lambda b, t: (b * nt + t, 0)),
        out_shape=jax.ShapeDtypeStruct((n, WIDTH), BF16),
        scratch_shapes=[
            pltpu.VMEM((tm + 2 * SUBLANES, 3 * WIDTH), F32),
            pltpu.VMEM((tm, 3 * WIDTH), F32),
            pltpu.VMEM((tm, LANES), F32),
            pltpu.VMEM((tm, LANES), F32),
            pltpu.VMEM((N_HEADS, HEAD_DIM, HEAD_DIM), F32),
            pltpu.VMEM((tm, WIDTH), F32),
            pltpu.VMEM((tm, WIDTH), BF16),
            pltpu.VMEM((tm, WIDTH), BF16),
            pltpu.VMEM((units, SUPER, SUPER), BF16),
            pltpu.VMEM((units, HEAD_DIM, SUPER), BF16),
            pltpu.VMEM((2 * units, SUBLANES, LANES), F32),
        ],
        compiler_params=pltpu.CompilerParams(
            dimension_semantics=("parallel", "arbitrary"), vmem_limit_bytes=VMEM_LIMIT),
        name="delta",
    )(proj, proj, proj, conv_w, lane_row(a_log, ALPHA_LANE), lane_row(dt_bias, ALPHA_LANE),
      norm_w.reshape(1, HEAD_DIM))


def _sg_body(u_ref, v_ref, z_ref, lnw_ref, lnb_ref, ws_ref, bs_ref, o_ref):
    tm = u_ref.shape[0]
    v = v_ref[...]
    mu = jnp.mean(v, axis=-1, keepdims=True)
    vc = v - mu
    var = jnp.mean(vc * vc, axis=-1, keepdims=True)
    vn = (vc * lax.rsqrt(var + LN_EPS) * lnw_ref[...] + lnb_ref[...]).astype(BF16)
    row = lax.broadcasted_iota(jnp.int32, (SG_CHUNK, SG_CHUNK), 0)
    col = lax.broadcasted_iota(jnp.int32, (SG_CHUNK, SG_CHUNK), 1)
    tril = col <= row
    gate = u_ref[...] * _silu(z_ref[...])
    bs = bs_ref[...]
    for g in range(N_HEADS):
        wg = jnp.where(tril, ws_ref[g], 0.0).astype(BF16)
        bg = bs[:, g:g + 1]
        for c in range(tm // SG_CHUNK):
            rs = slice(c * SG_CHUNK, (c + 1) * SG_CHUNK)
            cs = slice(g * HEAD_DIM, (g + 1) * HEAD_DIM)
            mixed = jnp.dot(wg, vn[rs, cs], preferred_element_type=F32) + bg
            o_ref[rs, cs] = (gate[rs, cs] * mixed).astype(o_ref.dtype)


def _sg_branch(proj, ln_w, ln_b, w_s, b_s, tm):
    n = proj.shape[0]
    tok = lambda col: pl.BlockSpec((tm, WIDTH), lambda i: (i, col))
    full = lambda shape: pl.BlockSpec(shape, lambda i: (0,) * len(shape))
    bs_t = jnp.zeros((SG_CHUNK, LANES), F32).at[:, :N_HEADS].set(b_s.T)
    return pl.pallas_call(
        _sg_body,
        grid=(n // tm,),
        in_specs=[
            tok(SG_OFF // WIDTH), tok(SG_OFF // WIDTH + 1), tok(SG_OFF // WIDTH + 2),
            full((1, WIDTH)), full((1, WIDTH)),
            full((N_HEADS, SG_CHUNK, SG_CHUNK)), full((SG_CHUNK, LANES)),
        ],
        out_specs=pl.BlockSpec((tm, WIDTH), lambda i: (i, 0)),
        out_shape=jax.ShapeDtypeStruct((n, WIDTH), BF16),
        compiler_params=pltpu.CompilerParams(
            dimension_semantics=("parallel",), vmem_limit_bytes=VMEM_LIMIT),
        name="spatial_gate",
    )(proj, proj, proj, ln_w.reshape(1, WIDTH), ln_b.reshape(1, WIDTH), w_s, bs_t)


LOG2E = 1.4426950408889634
ATTN_SUB = 256


def _fox_prep_body(q_ref, k_ref, v_ref, small_ref, fb_ref, qw_ref, kw_ref,
                   qn_ref, kn_ref, vt_ref, ccol_ref, crow_ref, carry_ref):
    t = pl.program_id(1)
    tm = q_ref.shape[0]

    @pl.when(t == 0)
    def _():
        carry_ref[...] = jnp.zeros(carry_ref.shape, F32)

    for h in range(N_HEADS):
        cs = slice(h * HEAD_DIM, (h + 1) * HEAD_DIM)
        q = q_ref[:, cs]
        k = k_ref[:, cs]
        q = q * lax.rsqrt(jnp.mean(q * q, axis=-1, keepdims=True) + NORM_EPS) * qw_ref[...]
        k = k * lax.rsqrt(jnp.mean(k * k, axis=-1, keepdims=True) + NORM_EPS) * kw_ref[...]
        qn_ref[:, cs] = (q * (HEAD_DIM ** -0.5 * LOG2E)).astype(BF16)
        kn_ref[:, cs] = k.astype(BF16)
        vt_ref[0, h, 0] = v_ref[:, cs].T.astype(BF16)

    logf = _log_sigmoid(small_ref[...] + fb_ref[...])
    row = lax.broadcasted_iota(jnp.int32, (tm, tm), 0)
    col = lax.broadcasted_iota(jnp.int32, (tm, tm), 1)
    c = _dot_exact_lhs(jnp.where(col <= row, 1.0, 0.0), logf) + carry_ref[0:1, :]
    carry_ref[0:1, :] = c[tm - 1:tm, :]
    c2 = c * LOG2E
    ct = c2.T
    for h in range(N_HEADS):
        lane = FORGET_LANE + h
        ccol_ref[:, h * HEAD_DIM:(h + 1) * HEAD_DIM] = jnp.broadcast_to(c2[:, lane:lane + 1], (tm, HEAD_DIM))
        crow_ref[h * SUBLANES:(h + 1) * SUBLANES, :] = jnp.broadcast_to(ct[lane:lane + 1, :], (SUBLANES, tm))


def _fox_prep(proj, f_bias, qn_w, kn_w, batch, seq, tm):
    n = proj.shape[0]
    nt = seq // tm
    tok = lambda width, col: pl.BlockSpec((tm, width), lambda b, t: (b * nt + t, col))
    full = lambda shape: pl.BlockSpec(shape, lambda b, t: (0,) * len(shape))
    fb = jnp.zeros((1, LANES), F32).at[0, FORGET_LANE:FORGET_LANE + N_HEADS].set(f_bias)
    out_tok = pl.BlockSpec((tm, WIDTH), lambda b, t: (b * nt + t, 0))
    return pl.pallas_call(
        _fox_prep_body,
        grid=(batch, nt),
        in_specs=[
            tok(WIDTH, FOX_OFF // WIDTH), tok(WIDTH, FOX_OFF // WIDTH + 1), tok(WIDTH, FOX_OFF // WIDTH + 2),
            tok(LANES, SMALL_OFF // LANES),
            full((1, LANES)), full((1, HEAD_DIM)), full((1, HEAD_DIM)),
        ],
        out_specs=[out_tok, out_tok,
                   pl.BlockSpec((1, N_HEADS, 1, HEAD_DIM, tm), lambda b, t: (b, 0, t, 0, 0)),
                   out_tok,
                   pl.BlockSpec((N_HEADS * SUBLANES, tm), lambda b, t: (b, t))],
        out_shape=[jax.ShapeDtypeStruct((n, WIDTH), BF16)] * 2
        + [jax.ShapeDtypeStruct((batch, N_HEADS, nt, HEAD_DIM, tm), BF16),
           jax.ShapeDtypeStruct((n, WIDTH), F32),
           jax.ShapeDtypeStruct((batch * N_HEADS * SUBLANES, seq), F32)],
        scratch_shapes=[pltpu.VMEM((SUBLANES, LANES), F32)],
        compiler_params=pltpu.CompilerParams(
            dimension_semantics=("parallel", "arbitrary"), vmem_limit_bytes=VMEM_LIMIT),
        name="fox_prep",
    )(proj, proj, proj, proj, fb, qn_w.reshape(1, HEAD_DIM), kn_w.reshape(1, HEAD_DIM))


def _fox_attn_body(q_ref, k_ref, vt_ref, cq_ref, ck_ref, z_ref, o_ref, m_ref, l_ref, acc_ref):
    i = pl.program_id(2)
    blk = q_ref.shape[0]
    m_ref[...] = jnp.full(m_ref.shape, -jnp.inf, F32)
    l_ref[...] = jnp.zeros(l_ref.shape, F32)
    acc_ref[...] = jnp.zeros(acc_ref.shape, F32)
    subs = [slice(s * ATTN_SUB, (s + 1) * ATTN_SUB) for s in range(blk // ATTN_SUB)]
    each = lambda f, *lists: [f(*args) for args in zip(*lists)]

    def kv_block(j, masked):
        rows = pl.ds(pl.multiple_of(j * blk, blk), blk)
        k = k_ref[rows, :]
        ck = jnp.tile(ck_ref[rows, :], (1, ATTN_SUB // HEAD_DIM))
        cq = [cq_ref[0:1, s] for s in subs]
        st = [lax.dot_general(k, q_ref[s, :], (((1,), (1,)), ((), ())), preferred_element_type=F32) - ck
              for s in subs]
        if masked:
            kpos = lax.broadcasted_iota(jnp.int32, (blk, ATTN_SUB), 0)
            qpos = lax.broadcasted_iota(jnp.int32, (blk, ATTN_SUB), 1)
            st = [jnp.where(kpos <= qpos + s.start, x, -jnp.inf) for s, x in zip(subs, st)]
        m_prev = [m_ref[0:1, s] for s in subs]
        m_new = each(lambda mp, x, c: jnp.maximum(mp, jnp.max(x, axis=0, keepdims=True) + c), m_prev, st, cq)
        alpha = each(lambda mp, mn: jnp.exp2(mp - mn), m_prev, m_new)
        p = each(lambda x, c, mn: jnp.exp2(x + (c - mn)), st, cq, m_new)
        v = vt_ref[0, 0, j]
        for s, a, pi, mn in zip(subs, alpha, p, m_new):
            l_ref[0:1, s] = a * l_ref[0:1, s] + jnp.sum(pi, axis=0, keepdims=True)
            acc_ref[:, s] = a * acc_ref[:, s] + jnp.dot(v, pi.astype(BF16), preferred_element_type=F32)
            m_ref[0:1, s] = mn

    def off_diagonal(j, carry):
        kv_block(j, False)
        return carry

    lax.fori_loop(0, i, off_diagonal, 0)
    kv_block(i, True)
    o = (acc_ref[...] / l_ref[0:1, :]).T
    o_ref[...] = (o * _silu(z_ref[...])).astype(o_ref.dtype)


def _fox_attention(qn, kn, vt, ccol, crow, proj, batch, seq, blk):
    n = qn.shape[0]
    nb = seq // blk
    assert vt.shape == (batch, N_HEADS, nb, HEAD_DIM, blk)
    qspec = lambda arr_col: pl.BlockSpec((blk, HEAD_DIM), lambda b, h, i: (b * nb + i, arr_col(h)))
    seq_spec = pl.BlockSpec((seq, HEAD_DIM), lambda b, h, i: (b, h))
    return pl.pallas_call(
        _fox_attn_body,
        grid=(batch, N_HEADS, nb),
        in_specs=[
            qspec(lambda h: h), seq_spec,
            pl.BlockSpec((1, 1, nb, HEAD_DIM, blk), lambda b, h, i: (b, h, 0, 0, 0)),
            pl.BlockSpec((SUBLANES, blk), lambda b, h, i: (b * N_HEADS + h, i)),
            seq_spec,
            qspec(lambda h: (FOX_OFF + 3 * WIDTH) // HEAD_DIM + h),
        ],
        out_specs=qspec(lambda h: h),
        out_shape=jax.ShapeDtypeStruct((n, WIDTH), BF16),
        scratch_shapes=[pltpu.VMEM((SUBLANES, blk), F32), pltpu.VMEM((SUBLANES, blk), F32),
                        pltpu.VMEM((HEAD_DIM, blk), F32)],
        compiler_params=pltpu.CompilerParams(
            dimension_semantics=("parallel", "parallel", "arbitrary"),
            vmem_limit_bytes=VMEM_LIMIT),
        name="fox_attention",
    )(qn, kn, vt, crow, ccol, proj)


def _merge_body(od_ref, os_ref, of_ref, g_ref, x_ref, wb_ref, wo_ref, o_ref):
    merged = None
    for nbr, br_ref in enumerate((od_ref, os_ref, of_ref)):
        up = jnp.dot(br_ref[...], wb_ref[nbr], preferred_element_type=F32)
        term = _sigmoid(g_ref[:, nbr * D_MODEL:(nbr + 1) * D_MODEL]) * up
        merged = term if merged is None else merged + term
    o_ref[...] = x_ref[...] + jnp.dot(merged.astype(BF16), wo_ref[...], preferred_element_type=F32)


def _merge(o_delta, o_sg, o_fox, proj, x2d, w_branch, w_out, layer, tm):
    n = x2d.shape[0]
    tok = lambda width, col: pl.BlockSpec((tm, width), lambda i: (i, col))
    return pl.pallas_call(
        _merge_body,
        grid=(n // tm,),
        in_specs=[
            tok(WIDTH, 0), tok(WIDTH, 0), tok(WIDTH, 0),
            tok(N_BRANCHES * D_MODEL, GATES_OFF // (N_BRANCHES * D_MODEL)),
            tok(D_MODEL, 0),
            pl.BlockSpec((None, N_BRANCHES, WIDTH, D_MODEL), lambda i: (layer, 0, 0, 0)),
            pl.BlockSpec((None, D_MODEL, D_MODEL), lambda i: (layer, 0, 0)),
        ],
        out_specs=tok(D_MODEL, 0),
        out_shape=jax.ShapeDtypeStruct((n, D_MODEL), F32),
        compiler_params=pltpu.CompilerParams(
            dimension_semantics=("parallel",), vmem_limit_bytes=VMEM_LIMIT),
        name="merge",
    )(o_delta, o_sg, o_fox, proj, x2d, w_branch, w_out)


def _final_norm_body(x_ref, w_ref, o_ref):
    x = x_ref[...]
    ms = jnp.mean(x * x, axis=-1, keepdims=True)
    o_ref[...] = x * lax.rsqrt(ms + NORM_EPS) * w_ref[...]


def _final_norm(x2d, w, tm):
    n = x2d.shape[0]
    return pl.pallas_call(
        _final_norm_body,
        grid=(n // tm,),
        in_specs=[pl.BlockSpec((tm, D_MODEL), lambda i: (i, 0)),
                  pl.BlockSpec((1, D_MODEL), lambda i: (0, 0))],
        out_specs=pl.BlockSpec((tm, D_MODEL), lambda i: (i, 0)),
        out_shape=jax.ShapeDtypeStruct((n, D_MODEL), F32),
        compiler_params=pltpu.CompilerParams(dimension_semantics=("parallel",)),
        name="final_norm",
    )(x2d, w.reshape(1, D_MODEL))


_IN_SIZES = (3 * WIDTH, WIDTH, N_HEADS, N_HEADS, 2 * WIDTH, WIDTH, 3 * WIDTH, WIDTH, N_HEADS,
             N_BRANCHES * D_MODEL)
_IN_STARTS = tuple(sum(_IN_SIZES[:i]) for i in range(len(_IN_SIZES) + 1))
D_IN = _IN_STARTS[-1]
_PACK_RUNS = (
    (_IN_STARTS[9], _IN_SIZES[9], GATES_OFF),
    (_IN_STARTS[0], _IN_SIZES[0] + _IN_SIZES[1], DN_OFF),
    (_IN_STARTS[4], _IN_SIZES[4] + _IN_SIZES[5], SG_OFF),
    (_IN_STARTS[6], _IN_SIZES[6] + _IN_SIZES[7], FOX_OFF),
    (_IN_STARTS[2], _IN_SIZES[2] + _IN_SIZES[3], SMALL_OFF + BETA_LANE),
    (_IN_STARTS[8], _IN_SIZES[8], SMALL_OFF + FORGET_LANE),
)


def _pack_body(w_ref, o_ref):
    rows = w_ref.shape[1]
    o_ref[0, :, SMALL_OFF:PACKED_COLS] = jnp.zeros((rows, PACKED_COLS - SMALL_OFF), BF16)
    for src, length, dst in _PACK_RUNS:
        o_ref[0, :, dst:dst + length] = w_ref[0, :, src:src + length].astype(BF16)


def _pack_w_in(w_in, rows):
    depth, d_model, d_in = w_in.shape
    assert d_in == D_IN and d_model % rows == 0
    return pl.pallas_call(
        _pack_body,
        grid=(depth, d_model // rows),
        in_specs=[pl.BlockSpec((1, rows, D_IN), lambda l, i: (l, i, 0))],
        out_specs=pl.BlockSpec((1, rows, PACKED_COLS), lambda l, i: (l, i, 0)),
        out_shape=jax.ShapeDtypeStruct((depth, d_model, PACKED_COLS), BF16),
        compiler_params=pltpu.CompilerParams(
            dimension_semantics=("parallel", "parallel"), vmem_limit_bytes=VMEM_LIMIT),
        name="pack_w_in",
    )(w_in)


def _tile(total, want):
    t = min(total, want)
    assert total % t == 0, (total, t)
    return t


def kernel(x, norm_w, w_in, f_bias, conv_w, a_log, dt_bias, dn_norm_w, sg_ln_w, sg_ln_b,
           w_spatial, b_spatial, fox_qnorm_w, fox_knorm_w, w_branch, w_out, final_norm_w):
    batch, seq, d_model = x.shape
    assert d_model == D_MODEL and seq % SUPER == 0
    n = batch * seq
    depth = w_in.shape[0]
    x2d = x.reshape(n, D_MODEL)
    tm_proj = _tile(n, 1024)
    tm_delta = _tile(seq, 256)
    tm_sg = _tile(n, 512)
    tm_prep = _tile(seq, 512)
    blk_attn = _tile(seq, 512)
    tm_merge = _tile(n, 512)
    w_packed = _pack_w_in(w_in, _tile(D_MODEL, 128))
    w_branch = w_branch.astype(BF16)
    w_out = w_out.astype(BF16)
    for l in range(depth):
        proj = _inproj(x2d, norm_w[l], w_packed, l, tm_proj)
        o_delta = _delta_branch(proj, conv_w[l], a_log[l], dt_bias[l], dn_norm_w[l], batch, seq, tm_delta)
        o_sg = _sg_branch(proj, sg_ln_w[l], sg_ln_b[l], w_spatial[l], b_spatial[l], tm_sg)
        qn, kn, vb, ccol, crow = _fox_prep(proj, f_bias[l], fox_qnorm_w[l], fox_knorm_w[l],
                                           batch, seq, tm_prep)
        o_fox = _fox_attention(qn, kn, vb, ccol, crow, proj, batch, seq, blk_attn)
        x2d = _merge(o_delta, o_sg, o_fox, proj, x2d, w_branch, w_out, l, tm_merge)
    return _final_norm(x2d, final_norm_w, _tile(n, 1024)).reshape(batch, seq, D_MODEL)
```

```python
import functools

import jax
import jax.numpy as jnp
from jax import lax
from jax.experimental import pallas as pl
from jax.experimental.pallas import tpu as pltpu

F32 = jnp.float32
BF16 = jnp.bfloat16

D_MODEL = 1024
HEAD_DIM = 128
N_HEADS = 4
WIDTH = N_HEADS * HEAD_DIM
N_BRANCHES = 3
DN_CHUNK = 64
SUPER = 2 * DN_CHUNK
CONV_WIDTH = 4
SG_CHUNK = 128
NORM_EPS = 1e-6
LN_EPS = 1e-5
L2_EPS = 1e-6
LANES = 128
SUBLANES = 8

GATES_OFF = 0
DN_OFF = GATES_OFF + N_BRANCHES * D_MODEL
SG_OFF = DN_OFF + 4 * WIDTH
FOX_OFF = SG_OFF + 3 * WIDTH
PACKED_COLS = FOX_OFF + 4 * WIDTH
PROJ_TILE_N = PACKED_COLS // 4
assert PROJ_TILE_N % LANES == 0
BETA_LANE, ALPHA_LANE, FORGET_LANE = 0, N_HEADS, 2 * N_HEADS

VMEM_LIMIT = 48 * 1024 * 1024


def _dot(a, b):
    return jnp.dot(a.astype(BF16), b.astype(BF16), preferred_element_type=F32)


def _dot_nt(a, b):
    return lax.dot_general(a.astype(BF16), b.astype(BF16), (((1,), (1,)), ((), ())),
                           preferred_element_type=F32)


def _dot_exact_lhs(a01, x):
    a = a01.astype(BF16)
    x1 = x.astype(BF16)
    r1 = x - x1.astype(F32)
    x2 = r1.astype(BF16)
    x3 = (r1 - x2.astype(F32)).astype(BF16)
    d = lambda p: jnp.dot(a, p, preferred_element_type=F32)
    return d(x1) + d(x2) + d(x3)


def _sigmoid(x):
    return 1.0 / (1.0 + jnp.exp(-x))


def _silu(x):
    return x * _sigmoid(x)


def _softplus(x):
    return jnp.maximum(x, 0.0) + jnp.log(1.0 + jnp.exp(-jnp.abs(x)))


def _log_sigmoid(x):
    return -_softplus(-x)


def _inproj_body(x_ref, nw_ref, w_ref, ws_ref, o_ref, small_ref, h_ref):
    @pl.when(pl.program_id(1) == 0)
    def _():
        x = x_ref[...]
        ms = jnp.mean(x * x, axis=-1, keepdims=True)
        h_ref[...] = (x * lax.rsqrt(ms + NORM_EPS) * nw_ref[...]).astype(BF16)
        small_ref[...] = jnp.dot(h_ref[...], ws_ref[...], preferred_element_type=F32)

    o_ref[...] = jnp.dot(h_ref[...], w_ref[...], preferred_element_type=F32).astype(o_ref.dtype)


def _inproj(x2d, norm_w, w_packed, w_small, layer, tm):
    n = x2d.shape[0]
    return pl.pallas_call(
        _inproj_body,
        grid=(n // tm, PACKED_COLS // PROJ_TILE_N),
        in_specs=[
            pl.BlockSpec((tm, D_MODEL), lambda i, j: (i, 0)),
            pl.BlockSpec((1, D_MODEL), lambda i, j: (0, 0)),
            pl.BlockSpec((None, D_MODEL, PROJ_TILE_N), lambda i, j: (layer, 0, j)),
            pl.BlockSpec((None, D_MODEL, LANES), lambda i, j: (layer, 0, 0)),
        ],
        out_specs=[pl.BlockSpec((tm, PROJ_TILE_N), lambda i, j: (i, j)),
                   pl.BlockSpec((tm, LANES), lambda i, j: (i, 0))],
        out_shape=[jax.ShapeDtypeStruct((n, PACKED_COLS), BF16),
                   jax.ShapeDtypeStruct((n, LANES), F32)],
        scratch_shapes=[pltpu.VMEM((tm, D_MODEL), BF16)],
        compiler_params=pltpu.CompilerParams(
            dimension_semantics=("parallel", "arbitrary"), vmem_limit_bytes=VMEM_LIMIT),
        name="inproj",
    )(x2d, norm_w.reshape(1, D_MODEL), w_packed, w_small)


def _unit_lower_inverse(lows, eye, m8, moffs):
    each = lambda f, *lists: [f(*args) for args in zip(*lists)]
    l8 = each(lambda low: jnp.where(m8, low, 0.0), lows)
    l8_2 = each(_dot, l8, l8)
    l8_3 = each(_dot, l8, l8_2)
    l8_4 = each(_dot, l8_2, l8_2)
    p1 = each(lambda a, b, c: eye - a + b - c, l8, l8_2, l8_3)
    x = each(lambda p, l4: p + _dot(p, l4), p1, l8_4)
    for moff in moffs:
        xl = each(lambda xi, low: _dot(xi, jnp.where(moff, low, 0.0)), x, lows)
        x = each(lambda xi, xli: xi - _dot(xli, xi), x, xl)
    return x


def _delta_body(qkv_ref, z_ref, small_ref, convw_ref, alog_ref, dtb_ref, nw_ref, o_ref,
                ext_ref, y_ref, bt_ref, gt_ref, s_ref, u_ref, w_ref, qd_ref, qk_ref, kdt_ref, gl_ref):
    t = pl.program_id(1)
    tm = qkv_ref.shape[0]
    halo = SUBLANES

    @pl.when(t == 0)
    def _():
        ext_ref[0:halo, :] = jnp.zeros((halo, 3 * WIDTH), F32)
        s_ref[...] = jnp.zeros(s_ref.shape, F32)

    @pl.when(t > 0)
    def _():
        ext_ref[0:halo, :] = ext_ref[tm:tm + halo, :]

    ext_ref[halo:halo + tm, :] = qkv_ref[...].astype(F32)
    acc = None
    for j in range(CONV_WIDTH):
        off = halo - (CONV_WIDTH - 1) + j
        term = convw_ref[j:j + 1, :] * ext_ref[off:off + tm, :]
        acc = term if acc is None else acc + term
    y_ref[...] = _silu(acc)

    sm = small_ref[...]
    bt_ref[...] = _sigmoid(sm)
    gt_ref[...] = -jnp.exp(alog_ref[...]) * _softplus(sm + dtb_ref[...])

    row = lax.broadcasted_iota(jnp.int32, (SUPER, SUPER), 0)
    col = lax.broadcasted_iota(jnp.int32, (SUPER, SUPER), 1)
    blk = lambda idx, size: jnp.right_shift(idx, size.bit_length() - 1)
    same_chunk = blk(row, DN_CHUNK) == blk(col, DN_CHUNK)
    causal = same_chunk & (col <= row)
    strict = same_chunk & (col < row)
    causal_ones = jnp.where(causal, 1.0, 0.0).astype(BF16)
    eye = jnp.where(row == col, 1.0, 0.0).astype(F32)
    m8 = blk(row, 8) == blk(col, 8)
    moffs = [(blk(row, 2 * s) == blk(col, 2 * s)) & (blk(row, s) != blk(col, s)) for s in (8, 16, 32)]
    first_rows = row < DN_CHUNK
    first_cols = col < DN_CHUNK
    nw = nw_ref[...]

    each = lambda f, *lists: [f(*args) for args in zip(*lists)]
    units = [(sc, h) for sc in range(tm // SUPER) for h in range(N_HEADS)]
    hcols = lambda h: slice(h * HEAD_DIM, (h + 1) * HEAD_DIM)
    lows, kbs, ks, qs, decs, egcs = [], [], [], [], [], []
    for sc in range(tm // SUPER):
        rows = slice(sc * SUPER, (sc + 1) * SUPER)
        gc = _dot_exact_lhs(causal_ones, gt_ref[rows, :])
        gct = gc.T
        bt = bt_ref[rows, :]
        egc = jnp.exp(gc)
        gl0 = gc[DN_CHUNK - 1:DN_CHUNK, :]
        gl1 = gc[SUPER - 1:SUPER, :]
        ekd = jnp.exp(jnp.where(first_rows, gl0, gl1) - gc)
        egl = (jnp.exp(gl0), jnp.exp(gl1))
        for h in range(N_HEADS):
            lane = slice(ALPHA_LANE + h, ALPHA_LANE + h + 1)
            q = y_ref[rows, h * HEAD_DIM:(h + 1) * HEAD_DIM]
            k = y_ref[rows, WIDTH + h * HEAD_DIM:WIDTH + (h + 1) * HEAD_DIM]
            q = q * lax.rsqrt(jnp.sum(q * q, axis=-1, keepdims=True) + L2_EPS) * (HEAD_DIM ** -0.5)
            k = k * lax.rsqrt(jnp.sum(k * k, axis=-1, keepdims=True) + L2_EPS)
            dec = jnp.exp(jnp.where(causal, gc[:, lane] - gct[lane, :], -jnp.inf))
            kb = k * bt[:, BETA_LANE + h:BETA_LANE + h + 1]
            qd_ref[rows, hcols(h)] = (q * egc[:, lane]).astype(BF16)
            kdt_ref[sc * N_HEADS + h] = (k * ekd[:, lane]).T.astype(BF16)
            for c in range(2):
                gl_ref[(sc * N_HEADS + h) * 2 + c] = jnp.broadcast_to(egl[c][:, lane], (SUBLANES, LANES))
            qs.append(q), ks.append(k), kbs.append(kb), decs.append(dec), egcs.append(egc[:, lane])
    lows = each(lambda kb, k, dec: jnp.where(strict, _dot_nt(kb, k) * dec, 0.0), kbs, ks, decs)
    qks = each(lambda q, k, dec: (_dot_nt(q, k) * dec).astype(BF16), qs, ks, decs)
    tmats = _unit_lower_inverse(lows, eye, m8, moffs)
    for (sc, h), tmat, kb, egc_col, qk in zip(units, tmats, kbs, egcs, qks):
        rows = slice(sc * SUPER, (sc + 1) * SUPER)
        v = y_ref[rows, 2 * WIDTH + h * HEAD_DIM:2 * WIDTH + (h + 1) * HEAD_DIM]
        u_ref[rows, hcols(h)] = _dot(tmat, v * bt_ref[rows, BETA_LANE + h:BETA_LANE + h + 1])
        w_ref[rows, hcols(h)] = _dot(tmat, kb * egc_col).astype(BF16)
        qk_ref[sc * N_HEADS + h] = qk

    heads = list(range(N_HEADS))

    def scan_step(sc, carry):
        rows = pl.ds(pl.multiple_of(sc * SUPER, SUPER), SUPER)
        unit = lambda h: sc * N_HEADS + h
        u = [u_ref[rows, hcols(h)] for h in heads]
        w = [w_ref[rows, hcols(h)] for h in heads]
        kdt = [kdt_ref[unit(h)] for h in heads]
        zero = jnp.zeros((HEAD_DIM, SUPER), BF16)
        kdt0 = [jnp.where(first_cols, x, zero) for x in kdt]
        kdt1 = [jnp.where(first_cols, zero, x) for x in kdt]
        g0 = [jnp.tile(gl_ref[unit(h) * 2], (HEAD_DIM // SUBLANES, 1)) for h in heads]
        g1 = [jnp.tile(gl_ref[unit(h) * 2 + 1], (HEAD_DIM // SUBLANES, 1)) for h in heads]
        s0 = [s_ref[h] for h in heads]
        vn0 = each(lambda ui, wi, si: ui - _dot(wi, si), u, w, s0)
        s1 = each(lambda si, gi, ki, vi: si * gi + _dot(ki, vi), s0, g0, kdt0, vn0)
        vn1 = each(lambda ui, wi, si: ui - _dot(wi, si), u, w, s1)
        s2 = each(lambda si, gi, ki, vi: si * gi + _dot(ki, vi), s1, g1, kdt1, vn1)
        for h in heads:
            s_ref[h] = s2[h]
            qd = qd_ref[rows, hcols(h)]
            vn = jnp.where(first_rows, vn0[h], vn1[h])
            o = jnp.where(first_rows, _dot(qd, s0[h]), _dot(qd, s1[h])) + _dot(qk_ref[unit(h)], vn)
            o = o * lax.rsqrt(jnp.mean(o * o, axis=-1, keepdims=True) + NORM_EPS) * nw
            o_ref[rows, hcols(h)] = (o * _silu(z_ref[rows, hcols(h)].astype(F32))).astype(o_ref.dtype)
        return carry

    lax.fori_loop(0, tm // SUPER, scan_step, 0)


def _delta_branch(proj, small, conv_w, a_log, dt_bias, norm_w, batch, seq, tm):
    n = proj.shape[0]
    nt = seq // tm
    units = (tm // SUPER) * N_HEADS
    lane_row = lambda vals, off: jnp.zeros((1, LANES), F32).at[0, off:off + N_HEADS].set(vals)
    tok = lambda width, col: pl.BlockSpec((tm, width), lambda b, t: (b * nt + t, col))
    full = lambda shape: pl.BlockSpec(shape, lambda b, t: (0,) * len(shape))
    return pl.pallas_call(
        _delta_body,
        grid=(batch, nt),
        in_specs=[
            tok(3 * WIDTH, DN_OFF // (3 * WIDTH)),
            tok(WIDTH, (DN_OFF + 3 * WIDTH) // WIDTH),
            tok(LANES, 0),
            full((CONV_WIDTH, 3 * WIDTH)),
            full((1, LANES)),
            full((1, LANES)),
            full((1, HEAD_DIM)),
        ],
        out_specs=pl.BlockSpec((tm, WIDTH), lambda b, t: (b * nt + t, 0)),
        out_shape=jax.ShapeDtypeStruct((n, WIDTH), BF16),
        scratch_shapes=[
            pltpu.VMEM((tm + 2 * SUBLANES, 3 * WIDTH), F32),
            pltpu.VMEM((tm, 3 * WIDTH), F32),
            pltpu.VMEM((tm, LANES), F32),
            pltpu.VMEM((tm, LANES), F32),
            pltpu.VMEM((N_HEADS, HEAD_DIM, HEAD_DIM), F32),
            pltpu.VMEM((tm, WIDTH), F32),
            pltpu.VMEM((tm, WIDTH), BF16),
            pltpu.VMEM((tm, WIDTH), BF16),
            pltpu.VMEM((units, SUPER, SUPER), BF16),
            pltpu.VMEM((units, HEAD_DIM, SUPER), BF16),
            pltpu.VMEM((2 * units, SUBLANES, LANES), F32),
        ],
        compiler_params=pltpu.CompilerParams(
            dimension_semantics=("parallel", "arbitrary"), vmem_limit_bytes=VMEM_LIMIT),
        name="delta",
    )(proj, proj, small, conv_w, lane_row(a_log, ALPHA_LANE), lane_row(dt_bias, ALPHA_LANE),
      norm_w.reshape(1, HEAD_DIM))


def _sg_body(u_ref, v_ref, z_ref, lnw_ref, lnb_ref, ws_ref, bs_ref, o_ref):
    tm = u_ref.shape[0]
    v = v_ref[...].astype(F32)
    mu = jnp.mean(v, axis=-1, keepdims=True)
    vc = v - mu
    var = jnp.mean(vc * vc, axis=-1, keepdims=True)
    vn = (vc * lax.rsqrt(var + LN_EPS) * lnw_ref[...] + lnb_ref[...]).astype(BF16)
    row = lax.broadcasted_iota(jnp.int32, (SG_CHUNK, SG_CHUNK), 0)
    col = lax.broadcasted_iota(jnp.int32, (SG_CHUNK, SG_CHUNK), 1)
    tril = col <= row
    gate = u_ref[...].astype(F32) * _silu(z_ref[...].astype(F32))
    bs = bs_ref[...]
    for g in range(N_HEADS):
        wg = jnp.where(tril, ws_ref[g], 0.0).astype(BF16)
        bg = bs[:, g:g + 1]
        for c in range(tm // SG_CHUNK):
            rs = slice(c * SG_CHUNK, (c + 1) * SG_CHUNK)
            cs = slice(g * HEAD_DIM, (g + 1) * HEAD_DIM)
            mixed = jnp.dot(wg, vn[rs, cs], preferred_element_type=F32) + bg
            o_ref[rs, cs] = (gate[rs, cs] * mixed).astype(o_ref.dtype)


def _sg_branch(proj, ln_w, ln_b, w_s, b_s, tm):
    n = proj.shape[0]
    tok = lambda col: pl.BlockSpec((tm, WIDTH), lambda i: (i, col))
    full = lambda shape: pl.BlockSpec(shape, lambda i: (0,) * len(shape))
    bs_t = jnp.zeros((SG_CHUNK, LANES), F32).at[:, :N_HEADS].set(b_s.T)
    return pl.pallas_call(
        _sg_body,
        grid=(n // tm,),
        in_specs=[
            tok(SG_OFF // WIDTH), tok(SG_OFF // WIDTH + 1), tok(SG_OFF // WIDTH + 2),
            full((1, WIDTH)), full((1, WIDTH)),
            full((N_HEADS, SG_CHUNK, SG_CHUNK)), full((SG_CHUNK, LANES)),
        ],
        out_specs=pl.BlockSpec((tm, WIDTH), lambda i: (i, 0)),
        out_shape=jax.ShapeDtypeStruct((n, WIDTH), BF16),
        compiler_params=pltpu.CompilerParams(
            dimension_semantics=("parallel",), vmem_limit_bytes=VMEM_LIMIT),
        name="spatial_gate",
    )(proj, proj, proj, ln_w.reshape(1, WIDTH), ln_b.reshape(1, WIDTH), w_s, bs_t)


LOG2E = 1.4426950408889634
ATTN_SUB = 256


def _fox_prep_body(q_ref, k_ref, v_ref, small_ref, fb_ref, qw_ref, kw_ref,
                   qn_ref, kn_ref, vt_ref, ccol_ref, crow_ref, carry_ref):
    t = pl.program_id(1)
    tm = q_ref.shape[0]

    @pl.when(t == 0)
    def _():
        carry_ref[...] = jnp.zeros(carry_ref.shape, F32)

    for h in range(N_HEADS):
        cs = slice(h * HEAD_DIM, (h + 1) * HEAD_DIM)
        q = q_ref[:, cs].astype(F32)
        k = k_ref[:, cs].astype(F32)
        q = q * lax.rsqrt(jnp.mean(q * q, axis=-1, keepdims=True) + NORM_EPS) * qw_ref[...]
        k = k * lax.rsqrt(jnp.mean(k * k, axis=-1, keepdims=True) + NORM_EPS) * kw_ref[...]
        qn_ref[:, cs] = (q * (HEAD_DIM ** -0.5 * LOG2E)).astype(BF16)
        kn_ref[:, cs] = k.astype(BF16)
        vt_ref[0, h, 0] = v_ref[:, cs].astype(F32).T.astype(BF16)

    logf = _log_sigmoid(small_ref[...] + fb_ref[...])
    row = lax.broadcasted_iota(jnp.int32, (tm, tm), 0)
    col = lax.broadcasted_iota(jnp.int32, (tm, tm), 1)
    c = _dot_exact_lhs(jnp.where(col <= row, 1.0, 0.0), logf) + carry_ref[0:1, :]
    carry_ref[0:1, :] = c[tm - 1:tm, :]
    c2 = c * LOG2E
    ct = c2.T
    for h in range(N_HEADS):
        lane = FORGET_LANE + h
        ccol_ref[:, h * HEAD_DIM:(h + 1) * HEAD_DIM] = jnp.broadcast_to(c2[:, lane:lane + 1], (tm, HEAD_DIM))
        crow_ref[h * SUBLANES:(h + 1) * SUBLANES, :] = jnp.broadcast_to(ct[lane:lane + 1, :], (SUBLANES, tm))


def _fox_prep(proj, small, f_bias, qn_w, kn_w, batch, seq, tm):
    n = proj.shape[0]
    nt = seq // tm
    tok = lambda width, col: pl.BlockSpec((tm, width), lambda b, t: (b * nt + t, col))
    full = lambda shape: pl.BlockSpec(shape, lambda b, t: (0,) * len(shape))
    fb = jnp.zeros((1, LANES), F32).at[0, FORGET_LANE:FORGET_LANE + N_HEADS].set(f_bias)
    out_tok = pl.BlockSpec((tm, WIDTH), lambda b, t: (b * nt + t, 0))
    return pl.pallas_call(
        _fox_prep_body,
        grid=(batch, nt),
        in_specs=[
            tok(WIDTH, FOX_OFF // WIDTH), tok(WIDTH, FOX_OFF // WIDTH + 1), tok(WIDTH, FOX_OFF // WIDTH + 2),
            tok(LANES, 0),
            full((1, LANES)), full((1, HEAD_DIM)), full((1, HEAD_DIM)),
        ],
        out_specs=[out_tok, out_tok,
                   pl.BlockSpec((1, N_HEADS, 1, HEAD_DIM, tm), lambda b, t: (b, 0, t, 0, 0)),
                   out_tok,
                   pl.BlockSpec((N_HEADS * SUBLANES, tm), lambda b, t: (b, t))],
        out_shape=[jax.ShapeDtypeStruct((n, WIDTH), BF16)] * 2
        + [jax.ShapeDtypeStruct((batch, N_HEADS, nt, HEAD_DIM, tm), BF16),
           jax.ShapeDtypeStruct((n, WIDTH), F32),
           jax.ShapeDtypeStruct((batch * N_HEADS * SUBLANES, seq), F32)],
        scratch_shapes=[pltpu.VMEM((SUBLANES, LANES), F32)],
        compiler_params=pltpu.CompilerParams(
            dimension_semantics=("parallel", "arbitrary"), vmem_limit_bytes=VMEM_LIMIT),
        name="fox_prep",
    )(proj, proj, proj, small, fb, qn_w.reshape(1, HEAD_DIM), kn_w.reshape(1, HEAD_DIM))


def _fox_attn_body(q_ref, k_ref, vt_ref, cq_ref, ck_ref, z_ref, o_ref,
                   m_ref, l_ref, acc_ref, sta_ref, stb_ref):
    i = pl.program_id(2)
    blk = q_ref.shape[0]
    m_ref[...] = jnp.full(m_ref.shape, -jnp.inf, F32)
    l_ref[...] = jnp.zeros(l_ref.shape, F32)
    acc_ref[...] = jnp.zeros(acc_ref.shape, F32)
    subs = [slice(s * ATTN_SUB, (s + 1) * ATTN_SUB) for s in range(blk // ATTN_SUB)]
    each = lambda f, *lists: [f(*args) for args in zip(*lists)]

    def scores(j, st_ref):
        rows = pl.ds(pl.multiple_of(j * blk, blk), blk)
        k = k_ref[rows, :]
        ck = jnp.tile(ck_ref[rows, :], (1, ATTN_SUB // HEAD_DIM))
        for si, s in enumerate(subs):
            st_ref[si] = lax.dot_general(k, q_ref[s, :], (((1,), (1,)), ((), ())),
                                         preferred_element_type=F32) - ck

    def softmax_pv(j, st_ref, masked):
        st = [st_ref[si] for si in range(len(subs))]
        cq = [cq_ref[0:1, s] for s in subs]
        if masked:
            kpos = lax.broadcasted_iota(jnp.int32, (blk, ATTN_SUB), 0)
            qpos = lax.broadcasted_iota(jnp.int32, (blk, ATTN_SUB), 1)
            st = [jnp.where(kpos <= qpos + s.start, x, -jnp.inf) for s, x in zip(subs, st)]
        m_prev = [m_ref[0:1, s] for s in subs]
        m_new = each(lambda mp, x, c: jnp.maximum(mp, jnp.max(x, axis=0, keepdims=True) + c), m_prev, st, cq)
        alpha = each(lambda mp, mn: jnp.exp2(mp - mn), m_prev, m_new)
        p = each(lambda x, c, mn: jnp.exp2(x + (c - mn)), st, cq, m_new)
        v = vt_ref[0, 0, j]
        for s, a, pi, mn in zip(subs, alpha, p, m_new):
            l_ref[0:1, s] = a * l_ref[0:1, s] + jnp.sum(pi, axis=0, keepdims=True)
            acc_ref[:, s] = a * acc_ref[:, s] + jnp.dot(v, pi.astype(BF16), preferred_element_type=F32)
            m_ref[0:1, s] = mn

    def step(j, cur_ref, nxt_ref):
        scores(j + 1, nxt_ref)
        softmax_pv(j, cur_ref, False)

    def pair(p, carry):
        step(2 * p, sta_ref, stb_ref)
        step(2 * p + 1, stb_ref, sta_ref)
        return carry

    scores(0, sta_ref)
    lax.fori_loop(0, i // 2, pair, 0)

    @pl.when(i % 2 == 0)
    def _():
        softmax_pv(i, sta_ref, True)

    @pl.when(i % 2 == 1)
    def _():
        step(i - 1, sta_ref, stb_ref)
        softmax_pv(i, stb_ref, True)

    o = (acc_ref[...] / l_ref[0:1, :]).T
    o_ref[...] = (o * _silu(z_ref[...].astype(F32))).astype(o_ref.dtype)


def _fox_attention(qn, kn, vt, ccol, crow, proj, batch, seq, blk):
    n = qn.shape[0]
    nb = seq // blk
    assert vt.shape == (batch, N_HEADS, nb, HEAD_DIM, blk)
    qspec = lambda arr_col: pl.BlockSpec((blk, HEAD_DIM), lambda b, h, i: (b * nb + i, arr_col(h)))
    seq_spec = pl.BlockSpec((seq, HEAD_DIM), lambda b, h, i: (b, h))
    return pl.pallas_call(
        _fox_attn_body,
        grid=(batch, N_HEADS, nb),
        in_specs=[
            qspec(lambda h: h), seq_spec,
            pl.BlockSpec((1, 1, nb, HEAD_DIM, blk), lambda b, h, i: (b, h, 0, 0, 0)),
            pl.BlockSpec((SUBLANES, blk), lambda b, h, i: (b * N_HEADS + h, i)),
            seq_spec,
            qspec(lambda h: (FOX_OFF + 3 * WIDTH) // HEAD_DIM + h),
        ],
        out_specs=qspec(lambda h: h),
        out_shape=jax.ShapeDtypeStruct((n, WIDTH), BF16),
        scratch_shapes=[pltpu.VMEM((SUBLANES, blk), F32), pltpu.VMEM((SUBLANES, blk), F32),
                        pltpu.VMEM((HEAD_DIM, blk), F32),
                        pltpu.VMEM((blk // ATTN_SUB, blk, ATTN_SUB), F32),
                        pltpu.VMEM((blk // ATTN_SUB, blk, ATTN_SUB), F32)],
        compiler_params=pltpu.CompilerParams(
            dimension_semantics=("parallel", "parallel", "arbitrary"),
            vmem_limit_bytes=VMEM_LIMIT),
        name="fox_attention",
    )(qn, kn, vt, crow, ccol, proj)


def _merge_body(od_ref, os_ref, of_ref, g_ref, x_ref, wb_ref, wo_ref, o_ref):
    merged = None
    for nbr, br_ref in enumerate((od_ref, os_ref, of_ref)):
        up = jnp.dot(br_ref[...], wb_ref[nbr], preferred_element_type=F32)
        term = _sigmoid(g_ref[:, nbr * D_MODEL:(nbr + 1) * D_MODEL].astype(F32)) * up
        merged = term if merged is None else merged + term
    o_ref[...] = x_ref[...] + jnp.dot(merged.astype(BF16), wo_ref[...], preferred_element_type=F32)


def _merge(o_delta, o_sg, o_fox, proj, x2d, w_branch, w_out, layer, tm):
    n = x2d.shape[0]
    tok = lambda width, col: pl.BlockSpec((tm, width), lambda i: (i, col))
    return pl.pallas_call(
        _merge_body,
        grid=(n // tm,),
        in_specs=[
            tok(WIDTH, 0), tok(WIDTH, 0), tok(WIDTH, 0),
            tok(N_BRANCHES * D_MODEL, GATES_OFF // (N_BRANCHES * D_MODEL)),
            tok(D_MODEL, 0),
            pl.BlockSpec((None, N_BRANCHES, WIDTH, D_MODEL), lambda i: (layer, 0, 0, 0)),
            pl.BlockSpec((None, D_MODEL, D_MODEL), lambda i: (layer, 0, 0)),
        ],
        out_specs=tok(D_MODEL, 0),
        out_shape=jax.ShapeDtypeStruct((n, D_MODEL), F32),
        compiler_params=pltpu.CompilerParams(
            dimension_semantics=("parallel",), vmem_limit_bytes=VMEM_LIMIT),
        name="merge",
    )(o_delta, o_sg, o_fox, proj, x2d, w_branch, w_out)


def _final_norm_body(x_ref, w_ref, o_ref):
    x = x_ref[...]
    ms = jnp.mean(x * x, axis=-1, keepdims=True)
    o_ref[...] = x * lax.rsqrt(ms + NORM_EPS) * w_ref[...]


def _final_norm(x2d, w, tm):
    n = x2d.shape[0]
    return pl.pallas_call(
        _final_norm_body,
        grid=(n // tm,),
        in_specs=[pl.BlockSpec((tm, D_MODEL), lambda i: (i, 0)),
                  pl.BlockSpec((1, D_MODEL), lambda i: (0, 0))],
        out_specs=pl.BlockSpec((tm, D_MODEL), lambda i: (i, 0)),
        out_shape=jax.ShapeDtypeStruct((n, D_MODEL), F32),
        compiler_params=pltpu.CompilerParams(dimension_semantics=("parallel",)),
        name="final_norm",
    )(x2d, w.reshape(1, D_MODEL))


_IN_SIZES = (3 * WIDTH, WIDTH, N_HEADS, N_HEADS, 2 * WIDTH, WIDTH, 3 * WIDTH, WIDTH, N_HEADS,
             N_BRANCHES * D_MODEL)
_IN_STARTS = tuple(sum(_IN_SIZES[:i]) for i in range(len(_IN_SIZES) + 1))
D_IN = _IN_STARTS[-1]
_PACK_RUNS = (
    (_IN_STARTS[9], _IN_SIZES[9], GATES_OFF),
    (_IN_STARTS[0], _IN_SIZES[0] + _IN_SIZES[1], DN_OFF),
    (_IN_STARTS[4], _IN_SIZES[4] + _IN_SIZES[5], SG_OFF),
    (_IN_STARTS[6], _IN_SIZES[6] + _IN_SIZES[7], FOX_OFF),
)
_SMALL_RUNS = (
    (_IN_STARTS[2], _IN_SIZES[2] + _IN_SIZES[3], BETA_LANE),
    (_IN_STARTS[8], _IN_SIZES[8], FORGET_LANE),
)


def _pack_body(w_ref, o_ref, small_ref):
    rows = w_ref.shape[1]
    for src, length, dst in _PACK_RUNS:
        o_ref[0, :, dst:dst + length] = w_ref[0, :, src:src + length].astype(BF16)
    small_ref[0] = jnp.zeros((rows, LANES), BF16)
    for src, length, dst in _SMALL_RUNS:
        small_ref[0, :, dst:dst + length] = w_ref[0, :, src:src + length].astype(BF16)


def _pack_w_in(w_in, rows):
    depth, d_model, d_in = w_in.shape
    assert d_in == D_IN and d_model % rows == 0
    return pl.pallas_call(
        _pack_body,
        grid=(depth, d_model // rows),
        in_specs=[pl.BlockSpec((1, rows, D_IN), lambda l, i: (l, i, 0))],
        out_specs=[pl.BlockSpec((1, rows, PACKED_COLS), lambda l, i: (l, i, 0)),
                   pl.BlockSpec((1, rows, LANES), lambda l, i: (l, i, 0))],
        out_shape=[jax.ShapeDtypeStruct((depth, d_model, PACKED_COLS), BF16),
                   jax.ShapeDtypeStruct((depth, d_model, LANES), BF16)],
        compiler_params=pltpu.CompilerParams(
            dimension_semantics=("parallel", "parallel"), vmem_limit_bytes=VMEM_LIMIT),
        name="pack_w_in",
    )(w_in)


def _tile(total, want):
    t = min(total, want)
    assert total % t == 0, (total, t)
    return t


def kernel(x, norm_w, w_in, f_bias, conv_w, a_log, dt_bias, dn_norm_w, sg_ln_w, sg_ln_b,
           w_spatial, b_spatial, fox_qnorm_w, fox_knorm_w, w_branch, w_out, final_norm_w):
    batch, seq, d_model = x.shape
    assert d_model == D_MODEL and seq % SUPER == 0
    n = batch * seq
    depth = w_in.shape[0]
    x2d = x.reshape(n, D_MODEL)
    tm_proj = _tile(n, 1024)
    tm_delta = _tile(seq, 256)
    tm_sg = _tile(n, 512)
    tm_prep = _tile(seq, 512)
    blk_attn = _tile(seq, 512)
    tm_merge = _tile(n, 512)
    w_packed, w_small = _pack_w_in(w_in, _tile(D_MODEL, 128))
    w_branch = w_branch.astype(BF16)
    w_out = w_out.astype(BF16)
    for l in range(depth):
        proj, small = _inproj(x2d, norm_w[l], w_packed, w_small, l, tm_proj)
        o_delta = _delta_branch(proj, small, conv_w[l], a_log[l], dt_bias[l], dn_norm_w[l],
                                batch, seq, tm_delta)
        o_sg = _sg_branch(proj, sg_ln_w[l], sg_ln_b[l], w_spatial[l], b_spatial[l], tm_sg)
        qn, kn, vt, ccol, crow = _fox_prep(proj, small, f_bias[l], fox_qnorm_w[l], fox_knorm_w[l],
                                           batch, seq, tm_prep)
        o_fox = _fox_attention(qn, kn, vt, ccol, crow, proj, batch, seq, blk_attn)
        x2d = _merge(o_delta, o_sg, o_fox, proj, x2d, w_branch, w_out, l, tm_merge)
    return _final_norm(x2d, final_norm_w, _tile(n, 1024)).reshape(batch, seq, D_MODEL)
```

```python
import jax
import jax.numpy as jnp
from jax import lax
from jax.experimental import pallas as pl
from jax.experimental.pallas import tpu as pltpu

F32 = jnp.float32
BF16 = jnp.bfloat16

D_MODEL = 1024
HEAD_DIM = 128
N_HEADS = 4
WIDTH = N_HEADS * HEAD_DIM
N_BRANCHES = 3
DN_CHUNK = 64
SUPER = 2 * DN_CHUNK
CONV_WIDTH = 4
SG_CHUNK = 128
NORM_EPS = 1e-6
LN_EPS = 1e-5
L2_EPS = 1e-6
LANES = 128
SUBLANES = 8

GATES_OFF = 0
DN_OFF = GATES_OFF + N_BRANCHES * D_MODEL
SG_OFF = DN_OFF + 4 * WIDTH
FOX_OFF = SG_OFF + 3 * WIDTH
PACKED_COLS = FOX_OFF + 4 * WIDTH
PROJ_TILE_N = PACKED_COLS // 4
assert PROJ_TILE_N % LANES == 0
BETA_LANE, ALPHA_LANE, FORGET_LANE = 0, N_HEADS, 2 * N_HEADS

VMEM_LIMIT = 48 * 1024 * 1024


def _dot(a, b):
    return jnp.dot(a.astype(BF16), b.astype(BF16), preferred_element_type=F32)


def _dot_nt(a, b):
    return lax.dot_general(a.astype(BF16), b.astype(BF16), (((1,), (1,)), ((), ())),
                           preferred_element_type=F32)


def _dot_exact_lhs(a01, x):
    a = a01.astype(BF16)
    x1 = x.astype(BF16)
    r1 = x - x1.astype(F32)
    x2 = r1.astype(BF16)
    x3 = (r1 - x2.astype(F32)).astype(BF16)
    d = lambda p: jnp.dot(a, p, preferred_element_type=F32)
    return d(x1) + d(x2) + d(x3)


def _sigmoid(x):
    return 1.0 / (1.0 + jnp.exp(-x))


def _silu(x):
    return x * _sigmoid(x)


def _softplus(x):
    return jnp.maximum(x, 0.0) + jnp.log(1.0 + jnp.exp(-jnp.abs(x)))


def _log_sigmoid(x):
    return -_softplus(-x)


def _rms_norm(x, w):
    return x * lax.rsqrt(jnp.mean(x * x, axis=-1, keepdims=True) + NORM_EPS) * w


def _inproj_body(h_ref, w_ref, ws_ref, o_ref, small_ref):
    @pl.when(pl.program_id(1) == 0)
    def _():
        small_ref[...] = jnp.dot(h_ref[...], ws_ref[...], preferred_element_type=F32)

    o_ref[...] = jnp.dot(h_ref[...], w_ref[...], preferred_element_type=F32).astype(o_ref.dtype)


def _inproj(h, w_packed, w_small, layer, tm):
    n = h.shape[0]
    return pl.pallas_call(
        _inproj_body,
        grid=(n // tm, PACKED_COLS // PROJ_TILE_N),
        in_specs=[
            pl.BlockSpec((tm, D_MODEL), lambda i, j: (i, 0)),
            pl.BlockSpec((None, D_MODEL, PROJ_TILE_N), lambda i, j: (layer, 0, j)),
            pl.BlockSpec((None, D_MODEL, LANES), lambda i, j: (layer, 0, 0)),
        ],
        out_specs=[pl.BlockSpec((tm, PROJ_TILE_N), lambda i, j: (i, j)),
                   pl.BlockSpec((tm, LANES), lambda i, j: (i, 0))],
        out_shape=[jax.ShapeDtypeStruct((n, PACKED_COLS), BF16),
                   jax.ShapeDtypeStruct((n, LANES), F32)],
        compiler_params=pltpu.CompilerParams(
            dimension_semantics=("parallel", "arbitrary"), vmem_limit_bytes=VMEM_LIMIT),
        name="inproj",
    )(h, w_packed, w_small)


def _unit_lower_inverse(lows, eye, m8, moffs):
    each = lambda f, *lists: [f(*args) for args in zip(*lists)]
    l8 = each(lambda low: jnp.where(m8, low, 0.0), lows)
    l8_2 = each(_dot, l8, l8)
    l8_3 = each(_dot, l8, l8_2)
    l8_4 = each(_dot, l8_2, l8_2)
    p1 = each(lambda a, b, c: eye - a + b - c, l8, l8_2, l8_3)
    x = each(lambda p, l4: p + _dot(p, l4), p1, l8_4)
    for moff in moffs:
        xl = each(lambda xi, low: _dot(xi, jnp.where(moff, low, 0.0)), x, lows)
        x = each(lambda xi, xli: xi - _dot(xli, xi), x, xl)
    return x


def _delta_body(qkv_ref, z_ref, small_ref, convw_ref, alog_ref, dtb_ref, nw_ref, o_ref,
                halo_ref, y_ref, bt_ref, gt_ref, s_ref, u_ref, w_ref, qd_ref, qk_ref, kdt_ref, gl_ref):
    t = pl.program_id(0)
    nb, tm = qkv_ref.shape[0], qkv_ref.shape[1]
    assert tm & (tm - 1) == 0
    nsc = tm // SUPER
    halo = SUBLANES
    taps = CONV_WIDTH - 1

    @pl.when(t == 0)
    def _():
        halo_ref[...] = jnp.zeros(halo_ref.shape, F32)
        s_ref[...] = jnp.zeros(s_ref.shape, F32)

    srow = lax.broadcasted_iota(jnp.int32, (taps * tm, tm), 0)
    scol = lax.broadcasted_iota(jnp.int32, (taps * tm, tm), 1)
    tap = jnp.right_shift(srow, tm.bit_length() - 1)
    shifts = jnp.where((srow - tap * tm) - scol == tap + 1, 1.0, 0.0).astype(BF16)
    for b in range(nb):
        xb = qkv_ref[b]
        xf = xb.astype(F32)
        shifted = jnp.dot(shifts, xb, preferred_element_type=F32)
        acc = convw_ref[taps:taps + 1, :] * xf
        head = None
        for s in range(1, taps + 1):
            wrow = convw_ref[taps - s:taps - s + 1, :]
            acc = acc + wrow * shifted[(s - 1) * tm:s * tm, :]
            patch = wrow * halo_ref[b, halo - s:2 * halo - s, :]
            head = patch if head is None else head + patch
        y_ref[b] = _silu(acc)
        y_ref[b, 0:halo, :] = _silu(acc[0:halo, :] + head)
        halo_ref[b, 0:halo, :] = xf[tm - halo:tm, :]
        sm = small_ref[b]
        bt_ref[b] = _sigmoid(sm)
        gt_ref[b] = -jnp.exp(alog_ref[...]) * _softplus(sm + dtb_ref[...])

    row = lax.broadcasted_iota(jnp.int32, (SUPER, SUPER), 0)
    col = lax.broadcasted_iota(jnp.int32, (SUPER, SUPER), 1)
    blk = lambda idx, size: jnp.right_shift(idx, size.bit_length() - 1)
    same_chunk = blk(row, DN_CHUNK) == blk(col, DN_CHUNK)
    causal = same_chunk & (col <= row)
    strict = same_chunk & (col < row)
    causal_ones = jnp.where(causal, 1.0, 0.0).astype(BF16)
    eye = jnp.where(row == col, 1.0, 0.0).astype(F32)
    m8 = blk(row, 8) == blk(col, 8)
    moffs = [(blk(row, 2 * s) == blk(col, 2 * s)) & (blk(row, s) != blk(col, s)) for s in (8, 16, 32)]
    first_rows = row < DN_CHUNK
    first_cols = col < DN_CHUNK
    nw = nw_ref[...]

    each = lambda f, *lists: [f(*args) for args in zip(*lists)]
    hcols = lambda h: slice(h * HEAD_DIM, (h + 1) * HEAD_DIM)
    srows = lambda sc: slice(sc * SUPER, (sc + 1) * SUPER)
    unit = lambda b, sc, h: (b * nsc + sc) * N_HEADS + h
    units = [(b, sc, h) for b in range(nb) for sc in range(nsc) for h in range(N_HEADS)]
    kbs, ks, qs, decs, egcs = [], [], [], [], []
    for b in range(nb):
        for sc in range(nsc):
            rows = srows(sc)
            gc = _dot_exact_lhs(causal_ones, gt_ref[b, rows, :])
            gct = gc.T
            bt = bt_ref[b, rows, :]
            egc = jnp.exp(gc)
            gl0 = gc[DN_CHUNK - 1:DN_CHUNK, :]
            gl1 = gc[SUPER - 1:SUPER, :]
            ekd = jnp.exp(jnp.where(first_rows, gl0, gl1) - gc)
            egl = (jnp.exp(gl0), jnp.exp(gl1))
            for h in range(N_HEADS):
                lane = slice(ALPHA_LANE + h, ALPHA_LANE + h + 1)
                q = y_ref[b, rows, h * HEAD_DIM:(h + 1) * HEAD_DIM]
                k = y_ref[b, rows, WIDTH + h * HEAD_DIM:WIDTH + (h + 1) * HEAD_DIM]
                q = q * lax.rsqrt(jnp.sum(q * q, axis=-1, keepdims=True) + L2_EPS) * (HEAD_DIM ** -0.5)
                k = k * lax.rsqrt(jnp.sum(k * k, axis=-1, keepdims=True) + L2_EPS)
                dec = jnp.exp(jnp.where(causal, gc[:, lane] - gct[lane, :], -jnp.inf))
                kb = k * bt[:, BETA_LANE + h:BETA_LANE + h + 1]
                qd_ref[b, rows, hcols(h)] = (q * egc[:, lane]).astype(BF16)
                kdt_ref[unit(b, sc, h)] = (k * ekd[:, lane]).T.astype(BF16)
                for c in range(2):
                    gl_ref[unit(b, sc, h) * 2 + c] = jnp.broadcast_to(egl[c][:, lane], (SUBLANES, LANES))
                qs.append(q), ks.append(k), kbs.append(kb), decs.append(dec), egcs.append(egc[:, lane])
    lows = each(lambda kb, k, dec: jnp.where(strict, _dot_nt(kb, k) * dec, 0.0), kbs, ks, decs)
    qks = each(lambda q, k, dec: (_dot_nt(q, k) * dec).astype(BF16), qs, ks, decs)
    tmats = _unit_lower_inverse(lows, eye, m8, moffs)
    for (b, sc, h), tmat, kb, egc_col, qk in zip(units, tmats, kbs, egcs, qks):
        rows = srows(sc)
        v = y_ref[b, rows, 2 * WIDTH + h * HEAD_DIM:2 * WIDTH + (h + 1) * HEAD_DIM]
        u_ref[b, rows, hcols(h)] = _dot(tmat, v * bt_ref[b, rows, BETA_LANE + h:BETA_LANE + h + 1])
        w_ref[b, rows, hcols(h)] = _dot(tmat, kb * egc_col).astype(BF16)
        qk_ref[unit(b, sc, h)] = qk

    chains = [(b, h) for b in range(nb) for h in range(N_HEADS)]
    zero = jnp.zeros((HEAD_DIM, SUPER), BF16)
    for sc in range(nsc):
        rows = srows(sc)
        u = [u_ref[b, rows, hcols(h)] for b, h in chains]
        w = [w_ref[b, rows, hcols(h)] for b, h in chains]
        kdt = [kdt_ref[unit(b, sc, h)] for b, h in chains]
        kdt0 = [jnp.where(first_cols, x, zero) for x in kdt]
        kdt1 = [jnp.where(first_cols, zero, x) for x in kdt]
        g0 = [jnp.tile(gl_ref[unit(b, sc, h) * 2], (HEAD_DIM // SUBLANES, 1)) for b, h in chains]
        g1 = [jnp.tile(gl_ref[unit(b, sc, h) * 2 + 1], (HEAD_DIM // SUBLANES, 1)) for b, h in chains]
        s0 = [s_ref[b * N_HEADS + h] for b, h in chains]
        vn0 = each(lambda ui, wi, si: ui - _dot(wi, si), u, w, s0)
        s1 = each(lambda si, gi, ki, vi: si * gi + _dot(ki, vi), s0, g0, kdt0, vn0)
        vn1 = each(lambda ui, wi, si: ui - _dot(wi, si), u, w, s1)
        s2 = each(lambda si, gi, ki, vi: si * gi + _dot(ki, vi), s1, g1, kdt1, vn1)
        for ci, (b, h) in enumerate(chains):
            s_ref[b * N_HEADS + h] = s2[ci]
        qd = [qd_ref[b, rows, hcols(h)] for b, h in chains]
        o0 = each(_dot, qd, s0)
        o1 = each(_dot, qd, s1)
        vn = each(lambda a, c: jnp.where(first_rows, a, c), vn0, vn1)
        oq = each(lambda qki, vi: _dot(qki, vi), [qk_ref[unit(b, sc, h)] for b, h in chains], vn)
        for ci, (b, h) in enumerate(chains):
            o = jnp.where(first_rows, o0[ci], o1[ci]) + oq[ci]
            o = o * lax.rsqrt(jnp.mean(o * o, axis=-1, keepdims=True) + NORM_EPS) * nw
            o_ref[b, rows, hcols(h)] = (o * _silu(z_ref[b, rows, hcols(h)].astype(F32))).astype(o_ref.dtype)


def _delta_branch(proj, small, conv_w, a_log, dt_bias, norm_w, batch, seq, tm):
    n = proj.shape[0]
    nt = seq // tm
    units = batch * (tm // SUPER) * N_HEADS
    lane_row = lambda vals, off: jnp.zeros((1, LANES), F32).at[0, off:off + N_HEADS].set(vals)
    tok = lambda width, col: pl.BlockSpec((batch, tm, width), lambda t: (0, t, col))
    full = lambda shape: pl.BlockSpec(shape, lambda t: (0,) * len(shape))
    proj3 = proj.reshape(batch, seq, proj.shape[1])
    out = pl.pallas_call(
        _delta_body,
        grid=(nt,),
        in_specs=[
            tok(3 * WIDTH, DN_OFF // (3 * WIDTH)),
            tok(WIDTH, (DN_OFF + 3 * WIDTH) // WIDTH),
            tok(LANES, 0),
            full((CONV_WIDTH, 3 * WIDTH)),
            full((1, LANES)),
            full((1, LANES)),
            full((1, HEAD_DIM)),
        ],
        out_specs=tok(WIDTH, 0),
        out_shape=jax.ShapeDtypeStruct((batch, seq, WIDTH), BF16),
        scratch_shapes=[
            pltpu.VMEM((batch, 2 * SUBLANES, 3 * WIDTH), F32),
            pltpu.VMEM((batch, tm, 3 * WIDTH), F32),
            pltpu.VMEM((batch, tm, LANES), F32),
            pltpu.VMEM((batch, tm, LANES), F32),
            pltpu.VMEM((batch * N_HEADS, HEAD_DIM, HEAD_DIM), F32),
            pltpu.VMEM((batch, tm, WIDTH), F32),
            pltpu.VMEM((batch, tm, WIDTH), BF16),
            pltpu.VMEM((batch, tm, WIDTH), BF16),
            pltpu.VMEM((units, SUPER, SUPER), BF16),
            pltpu.VMEM((units, HEAD_DIM, SUPER), BF16),
            pltpu.VMEM((2 * units, SUBLANES, LANES), F32),
        ],
        compiler_params=pltpu.CompilerParams(
            dimension_semantics=("arbitrary",), vmem_limit_bytes=VMEM_LIMIT),
        name="delta",
    )(proj3, proj3, small.reshape(batch, seq, LANES), conv_w,
      lane_row(a_log, ALPHA_LANE), lane_row(dt_bias, ALPHA_LANE), norm_w.reshape(1, HEAD_DIM))
    return out.reshape(n, WIDTH)


def _sg_body(u_ref, v_ref, z_ref, lnw_ref, lnb_ref, ws_ref, bs_ref, o_ref):
    tm = u_ref.shape[0]
    v = v_ref[...].astype(F32)
    mu = jnp.mean(v, axis=-1, keepdims=True)
    vc = v - mu
    var = jnp.mean(vc * vc, axis=-1, keepdims=True)
    vn = (vc * lax.rsqrt(var + LN_EPS) * lnw_ref[...] + lnb_ref[...]).astype(BF16)
    row = lax.broadcasted_iota(jnp.int32, (SG_CHUNK, SG_CHUNK), 0)
    col = lax.broadcasted_iota(jnp.int32, (SG_CHUNK, SG_CHUNK), 1)
    tril = col <= row
    gate = u_ref[...].astype(F32) * _silu(z_ref[...].astype(F32))
    bs = bs_ref[...]
    for g in range(N_HEADS):
        wg = jnp.where(tril, ws_ref[g], 0.0).astype(BF16)
        bg = bs[:, g:g + 1]
        for c in range(tm // SG_CHUNK):
            rs = slice(c * SG_CHUNK, (c + 1) * SG_CHUNK)
            cs = slice(g * HEAD_DIM, (g + 1) * HEAD_DIM)
            mixed = jnp.dot(wg, vn[rs, cs], preferred_element_type=F32) + bg
            o_ref[rs, cs] = (gate[rs, cs] * mixed).astype(o_ref.dtype)


def _sg_branch(proj, ln_w, ln_b, w_s, b_s, tm):
    n = proj.shape[0]
    tok = lambda col: pl.BlockSpec((tm, WIDTH), lambda i: (i, col))
    full = lambda shape: pl.BlockSpec(shape, lambda i: (0,) * len(shape))
    bs_t = jnp.zeros((SG_CHUNK, LANES), F32).at[:, :N_HEADS].set(b_s.T)
    return pl.pallas_call(
        _sg_body,
        grid=(n // tm,),
        in_specs=[
            tok(SG_OFF // WIDTH), tok(SG_OFF // WIDTH + 1), tok(SG_OFF // WIDTH + 2),
            full((1, WIDTH)), full((1, WIDTH)),
            full((N_HEADS, SG_CHUNK, SG_CHUNK)), full((SG_CHUNK, LANES)),
        ],
        out_specs=pl.BlockSpec((tm, WIDTH), lambda i: (i, 0)),
        out_shape=jax.ShapeDtypeStruct((n, WIDTH), BF16),
        compiler_params=pltpu.CompilerParams(
            dimension_semantics=("parallel",), vmem_limit_bytes=VMEM_LIMIT),
        name="spatial_gate",
    )(proj, proj, proj, ln_w.reshape(1, WIDTH), ln_b.reshape(1, WIDTH), w_s, bs_t)


LOG2E = 1.4426950408889634
ATTN_SUB = 256


def _fox_prep_body(q_ref, k_ref, v_ref, small_ref, fb_ref, qw_ref, kw_ref,
                   qn_ref, kn_ref, vt_ref, ccol_ref, crow_ref, carry_ref):
    t = pl.program_id(1)
    tm = q_ref.shape[0]

    @pl.when(t == 0)
    def _():
        carry_ref[...] = jnp.zeros(carry_ref.shape, F32)

    for h in range(N_HEADS):
        cs = slice(h * HEAD_DIM, (h + 1) * HEAD_DIM)
        q = q_ref[:, cs].astype(F32)
        k = k_ref[:, cs].astype(F32)
        q = q * lax.rsqrt(jnp.mean(q * q, axis=-1, keepdims=True) + NORM_EPS) * qw_ref[...]
        k = k * lax.rsqrt(jnp.mean(k * k, axis=-1, keepdims=True) + NORM_EPS) * kw_ref[...]
        qn_ref[:, cs] = (q * (HEAD_DIM ** -0.5 * LOG2E)).astype(BF16)
        kn_ref[:, cs] = k.astype(BF16)
        vt_ref[0, h, 0] = v_ref[:, cs].astype(F32).T.astype(BF16)

    logf = _log_sigmoid(small_ref[...] + fb_ref[...])
    row = lax.broadcasted_iota(jnp.int32, (tm, tm), 0)
    col = lax.broadcasted_iota(jnp.int32, (tm, tm), 1)
    c = _dot_exact_lhs(jnp.where(col <= row, 1.0, 0.0), logf) + carry_ref[0:1, :]
    carry_ref[0:1, :] = c[tm - 1:tm, :]
    c2 = c * LOG2E
    ct = c2.T
    for h in range(N_HEADS):
        lane = FORGET_LANE + h
        ccol_ref[:, h * HEAD_DIM:(h + 1) * HEAD_DIM] = jnp.broadcast_to(c2[:, lane:lane + 1], (tm, HEAD_DIM))
        crow_ref[h * SUBLANES:(h + 1) * SUBLANES, :] = jnp.broadcast_to(ct[lane:lane + 1, :], (SUBLANES, tm))


def _fox_prep(proj, small, f_bias, qn_w, kn_w, batch, seq, tm):
    n = proj.shape[0]
    nt = seq // tm
    tok = lambda width, col: pl.BlockSpec((tm, width), lambda b, t: (b * nt + t, col))
    full = lambda shape: pl.BlockSpec(shape, lambda b, t: (0,) * len(shape))
    fb = jnp.zeros((1, LANES), F32).at[0, FORGET_LANE:FORGET_LANE + N_HEADS].set(f_bias)
    out_tok = pl.BlockSpec((tm, WIDTH), lambda b, t: (b * nt + t, 0))
    return pl.pallas_call(
        _fox_prep_body,
        grid=(batch, nt),
        in_specs=[
            tok(WIDTH, FOX_OFF // WIDTH), tok(WIDTH, FOX_OFF // WIDTH + 1), tok(WIDTH, FOX_OFF // WIDTH + 2),
            tok(LANES, 0),
            full((1, LANES)), full((1, HEAD_DIM)), full((1, HEAD_DIM)),
        ],
        out_specs=[out_tok, out_tok,
                   pl.BlockSpec((1, N_HEADS, 1, HEAD_DIM, tm), lambda b, t: (b, 0, t, 0, 0)),
                   out_tok,
                   pl.BlockSpec((N_HEADS * SUBLANES, tm), lambda b, t: (b, t))],
        out_shape=[jax.ShapeDtypeStruct((n, WIDTH), BF16)] * 2
        + [jax.ShapeDtypeStruct((batch, N_HEADS, nt, HEAD_DIM, tm), BF16),
           jax.ShapeDtypeStruct((n, WIDTH), F32),
           jax.ShapeDtypeStruct((batch * N_HEADS * SUBLANES, seq), F32)],
        scratch_shapes=[pltpu.VMEM((SUBLANES, LANES), F32)],
        compiler_params=pltpu.CompilerParams(
            dimension_semantics=("parallel", "arbitrary"), vmem_limit_bytes=VMEM_LIMIT),
        name="fox_prep",
    )(proj, proj, proj, small, fb, qn_w.reshape(1, HEAD_DIM), kn_w.reshape(1, HEAD_DIM))


def _fox_attn_body(q_ref, k_ref, vt_ref, cq_ref, ck_ref, z_ref, o_ref,
                   m_ref, l_ref, acc_ref, sta_ref, stb_ref):
    i = pl.program_id(2)
    blk = q_ref.shape[0]
    m_ref[...] = jnp.full(m_ref.shape, -jnp.inf, F32)
    l_ref[...] = jnp.zeros(l_ref.shape, F32)
    acc_ref[...] = jnp.zeros(acc_ref.shape, F32)
    subs = [slice(s * ATTN_SUB, (s + 1) * ATTN_SUB) for s in range(blk // ATTN_SUB)]
    each = lambda f, *lists: [f(*args) for args in zip(*lists)]

    def scores(j, st_ref):
        rows = pl.ds(pl.multiple_of(j * blk, blk), blk)
        k = k_ref[rows, :]
        ck = jnp.tile(ck_ref[rows, :], (1, ATTN_SUB // HEAD_DIM))
        for si, s in enumerate(subs):
            st_ref[si] = lax.dot_general(k, q_ref[s, :], (((1,), (1,)), ((), ())),
                                         preferred_element_type=F32) - ck

    def softmax_pv(j, st_ref, masked):
        st = [st_ref[si] for si in range(len(subs))]
        cq = [cq_ref[0:1, s] for s in subs]
        if masked:
            kpos = lax.broadcasted_iota(jnp.int32, (blk, ATTN_SUB), 0)
            qpos = lax.broadcasted_iota(jnp.int32, (blk, ATTN_SUB), 1)
            st = [jnp.where(kpos <= qpos + s.start, x, -jnp.inf) for s, x in zip(subs, st)]
        m_prev = [m_ref[0:1, s] for s in subs]
        m_new = each(lambda mp, x, c: jnp.maximum(mp, jnp.max(x, axis=0, keepdims=True) + c), m_prev, st, cq)
        alpha = each(lambda mp, mn: jnp.exp2(mp - mn), m_prev, m_new)
        p = each(lambda x, c, mn: jnp.exp2(x + (c - mn)), st, cq, m_new)
        v = vt_ref[0, 0, j]
        for s, a, pi, mn in zip(subs, alpha, p, m_new):
            l_ref[0:1, s] = a * l_ref[0:1, s] + jnp.sum(pi, axis=0, keepdims=True)
            acc_ref[:, s] = a * acc_ref[:, s] + jnp.dot(v, pi.astype(BF16), preferred_element_type=F32)
            m_ref[0:1, s] = mn

    def step(j, cur_ref, nxt_ref):
        scores(j + 1, nxt_ref)
        softmax_pv(j, cur_ref, False)

    def pair(p, carry):
        step(2 * p, sta_ref, stb_ref)
        step(2 * p + 1, stb_ref, sta_ref)
        return carry

    scores(0, sta_ref)
    lax.fori_loop(0, i // 2, pair, 0)

    @pl.when(i % 2 == 0)
    def _():
        softmax_pv(i, sta_ref, True)

    @pl.when(i % 2 == 1)
    def _():
        step(i - 1, sta_ref, stb_ref)
        softmax_pv(i, stb_ref, True)

    o = (acc_ref[...] / l_ref[0:1, :]).T
    o_ref[...] = (o * _silu(z_ref[...].astype(F32))).astype(o_ref.dtype)


def _fox_attention(qn, kn, vt, ccol, crow, proj, batch, seq, blk):
    n = qn.shape[0]
    nb = seq // blk
    assert vt.shape == (batch, N_HEADS, nb, HEAD_DIM, blk)
    qspec = lambda arr_col: pl.BlockSpec((blk, HEAD_DIM), lambda b, h, i: (b * nb + i, arr_col(h)))
    seq_spec = pl.BlockSpec((seq, HEAD_DIM), lambda b, h, i: (b, h))
    return pl.pallas_call(
        _fox_attn_body,
        grid=(batch, N_HEADS, nb),
        in_specs=[
            qspec(lambda h: h), seq_spec,
            pl.BlockSpec((1, 1, nb, HEAD_DIM, blk), lambda b, h, i: (b, h, 0, 0, 0)),
            pl.BlockSpec((SUBLANES, blk), lambda b, h, i: (b * N_HEADS + h, i)),
            seq_spec,
            qspec(lambda h: (FOX_OFF + 3 * WIDTH) // HEAD_DIM + h),
        ],
        out_specs=qspec(lambda h: h),
        out_shape=jax.ShapeDtypeStruct((n, WIDTH), BF16),
        scratch_shapes=[pltpu.VMEM((SUBLANES, blk), F32), pltpu.VMEM((SUBLANES, blk), F32),
                        pltpu.VMEM((HEAD_DIM, blk), F32),
                        pltpu.VMEM((blk // ATTN_SUB, blk, ATTN_SUB), F32),
                        pltpu.VMEM((blk // ATTN_SUB, blk, ATTN_SUB), F32)],
        compiler_params=pltpu.CompilerParams(
            dimension_semantics=("parallel", "parallel", "arbitrary"),
            vmem_limit_bytes=VMEM_LIMIT),
        name="fox_attention",
    )(qn, kn, vt, crow, ccol, proj)


def _merge_body(od_ref, os_ref, of_ref, g_ref, x_ref, wb_ref, wo_ref, nw_ref, *out_refs):
    merged = None
    for nbr, br_ref in enumerate((od_ref, os_ref, of_ref)):
        up = jnp.dot(br_ref[...], wb_ref[nbr], preferred_element_type=F32)
        term = _sigmoid(g_ref[:, nbr * D_MODEL:(nbr + 1) * D_MODEL].astype(F32)) * up
        merged = term if merged is None else merged + term
    x_new = x_ref[...] + jnp.dot(merged.astype(BF16), wo_ref[...], preferred_element_type=F32)
    normed_ref = out_refs[-1]
    normed_ref[...] = _rms_norm(x_new, nw_ref[...]).astype(normed_ref.dtype)
    if len(out_refs) == 2:
        out_refs[0][...] = x_new


def _merge(o_delta, o_sg, o_fox, proj, x2d, w_branch, w_out, next_norm_w, layer, last, tm):
    n = x2d.shape[0]
    tok = lambda width, col: pl.BlockSpec((tm, width), lambda i: (i, col))
    if last:
        out_specs = [tok(D_MODEL, 0)]
        out_shape = [jax.ShapeDtypeStruct((n, D_MODEL), F32)]
    else:
        out_specs = [tok(D_MODEL, 0), tok(D_MODEL, 0)]
        out_shape = [jax.ShapeDtypeStruct((n, D_MODEL), F32), jax.ShapeDtypeStruct((n, D_MODEL), BF16)]
    return pl.pallas_call(
        _merge_body,
        grid=(n // tm,),
        in_specs=[
            tok(WIDTH, 0), tok(WIDTH, 0), tok(WIDTH, 0),
            tok(N_BRANCHES * D_MODEL, GATES_OFF // (N_BRANCHES * D_MODEL)),
            tok(D_MODEL, 0),
            pl.BlockSpec((None, N_BRANCHES, WIDTH, D_MODEL), lambda i: (layer, 0, 0, 0)),
            pl.BlockSpec((None, D_MODEL, D_MODEL), lambda i: (layer, 0, 0)),
            pl.BlockSpec((1, D_MODEL), lambda i: (0, 0)),
        ],
        out_specs=out_specs,
        out_shape=out_shape,
        compiler_params=pltpu.CompilerParams(
            dimension_semantics=("parallel",), vmem_limit_bytes=VMEM_LIMIT),
        name="merge",
    )(o_delta, o_sg, o_fox, proj, x2d, w_branch, w_out, next_norm_w.reshape(1, D_MODEL))


def _first_norm_body(x_ref, w_ref, o_ref):
    o_ref[...] = _rms_norm(x_ref[...], w_ref[...]).astype(o_ref.dtype)


def _first_norm(x2d, w, tm):
    n = x2d.shape[0]
    return pl.pallas_call(
        _first_norm_body,
        grid=(n // tm,),
        in_specs=[pl.BlockSpec((tm, D_MODEL), lambda i: (i, 0)),
                  pl.BlockSpec((1, D_MODEL), lambda i: (0, 0))],
        out_specs=pl.BlockSpec((tm, D_MODEL), lambda i: (i, 0)),
        out_shape=jax.ShapeDtypeStruct((n, D_MODEL), BF16),
        compiler_params=pltpu.CompilerParams(dimension_semantics=("parallel",)),
        name="first_norm",
    )(x2d, w.reshape(1, D_MODEL))


_IN_SIZES = (3 * WIDTH, WIDTH, N_HEADS, N_HEADS, 2 * WIDTH, WIDTH, 3 * WIDTH, WIDTH, N_HEADS,
             N_BRANCHES * D_MODEL)
_IN_STARTS = tuple(sum(_IN_SIZES[:i]) for i in range(len(_IN_SIZES) + 1))
D_IN = _IN_STARTS[-1]
_PACK_RUNS = (
    (_IN_STARTS[9], _IN_SIZES[9], GATES_OFF),
    (_IN_STARTS[0], _IN_SIZES[0] + _IN_SIZES[1], DN_OFF),
    (_IN_STARTS[4], _IN_SIZES[4] + _IN_SIZES[5], SG_OFF),
    (_IN_STARTS[6], _IN_SIZES[6] + _IN_SIZES[7], FOX_OFF),
)
_SMALL_RUNS = (
    (_IN_STARTS[2], _IN_SIZES[2] + _IN_SIZES[3], BETA_LANE),
    (_IN_STARTS[8], _IN_SIZES[8], FORGET_LANE),
)


def _pack_body(w_ref, o_ref, small_ref):
    rows = w_ref.shape[1]
    for src, length, dst in _PACK_RUNS:
        o_ref[0, :, dst:dst + length] = w_ref[0, :, src:src + length].astype(BF16)
    small_ref[0] = jnp.zeros((rows, LANES), BF16)
    for src, length, dst in _SMALL_RUNS:
        small_ref[0, :, dst:dst + length] = w_ref[0, :, src:src + length].astype(BF16)


def _pack_w_in(w_in, rows):
    depth, d_model, d_in = w_in.shape
    assert d_in == D_IN and d_model % rows == 0
    return pl.pallas_call(
        _pack_body,
        grid=(depth, d_model // rows),
        in_specs=[pl.BlockSpec((1, rows, D_IN), lambda l, i: (l, i, 0))],
        out_specs=[pl.BlockSpec((1, rows, PACKED_COLS), lambda l, i: (l, i, 0)),
                   pl.BlockSpec((1, rows, LANES), lambda l, i: (l, i, 0))],
        out_shape=[jax.ShapeDtypeStruct((depth, d_model, PACKED_COLS), BF16),
                   jax.ShapeDtypeStruct((depth, d_model, LANES), BF16)],
        compiler_params=pltpu.CompilerParams(
            dimension_semantics=("parallel", "parallel"), vmem_limit_bytes=VMEM_LIMIT),
        name="pack_w_in",
    )(w_in)


def _tile(total, want):
    t = min(total, want)
    assert total % t == 0, (total, t)
    return t


def kernel(x, norm_w, w_in, f_bias, conv_w, a_log, dt_bias, dn_norm_w, sg_ln_w, sg_ln_b,
           w_spatial, b_spatial, fox_qnorm_w, fox_knorm_w, w_branch, w_out, final_norm_w):
    batch, seq, d_model = x.shape
    assert d_model == D_MODEL and seq % SUPER == 0
    n = batch * seq
    depth = w_in.shape[0]
    x2d = x.reshape(n, D_MODEL)
    tm_proj = _tile(n, 2048)
    tm_delta = _tile(seq, SUPER)
    tm_sg = _tile(n, 512)
    tm_prep = _tile(seq, 512)
    blk_attn = _tile(seq, 512)
    tm_merge = _tile(n, 512)
    w_packed, w_small = _pack_w_in(w_in, _tile(D_MODEL, 128))
    w_branch = w_branch.astype(BF16)
    w_out = w_out.astype(BF16)
    h = _first_norm(x2d, norm_w[0], _tile(n, 1024))
    for l in range(depth):
        last = l == depth - 1
        proj, small = _inproj(h, w_packed, w_small, l, tm_proj)
        o_delta = _delta_branch(proj, small, conv_w[l], a_log[l], dt_bias[l], dn_norm_w[l],
                                batch, seq, tm_delta)
        o_sg = _sg_branch(proj, sg_ln_w[l], sg_ln_b[l], w_spatial[l], b_spatial[l], tm_sg)
        qn, kn, vt, ccol, crow = _fox_prep(proj, small, f_bias[l], fox_qnorm_w[l], fox_knorm_w[l],
                                           batch, seq, tm_prep)
        o_fox = _fox_attention(qn, kn, vt, ccol, crow, proj, batch, seq, blk_attn)
        outs = _merge(o_delta, o_sg, o_fox, proj, x2d, w_branch, w_out,
                      final_norm_w if last else norm_w[l + 1], l, last, tm_merge)
        if last:
            return outs[0].reshape(batch, seq, D_MODEL)
        x2d, h = outs
```

```python
import jax
import jax.numpy as jnp
from jax import lax
from jax.experimental import pallas as pl
from jax.experimental.pallas import tpu as pltpu

F32 = jnp.float32
BF16 = jnp.bfloat16

D_MODEL = 1024
HEAD_DIM = 128
N_HEADS = 4
WIDTH = N_HEADS * HEAD_DIM
N_BRANCHES = 3
DN_CHUNK = 64
SUPER = 2 * DN_CHUNK
CONV_WIDTH = 4
SG_CHUNK = 128
NORM_EPS = 1e-6
LN_EPS = 1e-5
L2_EPS = 1e-6
LANES = 128
SUBLANES = 8

GATES_OFF = 0
DN_OFF = GATES_OFF + N_BRANCHES * D_MODEL
SG_OFF = DN_OFF + 4 * WIDTH
FOX_OFF = SG_OFF + 3 * WIDTH
PACKED_COLS = FOX_OFF + 4 * WIDTH
PROJ_TILE_N = PACKED_COLS // 4
assert PROJ_TILE_N % LANES == 0
BETA_LANE, ALPHA_LANE, FORGET_LANE = 0, N_HEADS, 2 * N_HEADS

VMEM_LIMIT = 48 * 1024 * 1024


def _dot(a, b):
    return jnp.dot(a.astype(BF16), b.astype(BF16), preferred_element_type=F32)


def _dot_nt(a, b):
    return lax.dot_general(a.astype(BF16), b.astype(BF16), (((1,), (1,)), ((), ())),
                           preferred_element_type=F32)


def _dot_exact_lhs(a01, x):
    a = a01.astype(BF16)
    x1 = x.astype(BF16)
    r1 = x - x1.astype(F32)
    x2 = r1.astype(BF16)
    x3 = (r1 - x2.astype(F32)).astype(BF16)
    d = lambda p: jnp.dot(a, p, preferred_element_type=F32)
    return d(x1) + d(x2) + d(x3)


def _sigmoid(x):
    return 1.0 / (1.0 + jnp.exp(-x))


def _silu(x):
    return x * _sigmoid(x)


def _softplus(x):
    return jnp.maximum(x, 0.0) + jnp.log(1.0 + jnp.exp(-jnp.abs(x)))


def _log_sigmoid(x):
    return -_softplus(-x)


def _rms_norm(x, w):
    return x * lax.rsqrt(jnp.mean(x * x, axis=-1, keepdims=True) + NORM_EPS) * w


def _inproj_body(h_ref, w_ref, ws_ref, o_ref, small_ref):
    @pl.when(pl.program_id(1) == 0)
    def _():
        small_ref[...] = jnp.dot(h_ref[...], ws_ref[...], preferred_element_type=F32)

    o_ref[...] = jnp.dot(h_ref[...], w_ref[...], preferred_element_type=F32).astype(o_ref.dtype)


def _inproj(h, w_packed, w_small, layer, tm):
    n = h.shape[0]
    return pl.pallas_call(
        _inproj_body,
        grid=(n // tm, PACKED_COLS // PROJ_TILE_N),
        in_specs=[
            pl.BlockSpec((tm, D_MODEL), lambda i, j: (i, 0)),
            pl.BlockSpec((None, D_MODEL, PROJ_TILE_N), lambda i, j: (layer, 0, j)),
            pl.BlockSpec((None, D_MODEL, LANES), lambda i, j: (layer, 0, 0)),
        ],
        out_specs=[pl.BlockSpec((tm, PROJ_TILE_N), lambda i, j: (i, j)),
                   pl.BlockSpec((tm, LANES), lambda i, j: (i, 0))],
        out_shape=[jax.ShapeDtypeStruct((n, PACKED_COLS), BF16),
                   jax.ShapeDtypeStruct((n, LANES), F32)],
        compiler_params=pltpu.CompilerParams(
            dimension_semantics=("parallel", "arbitrary"), vmem_limit_bytes=VMEM_LIMIT),
        name="inproj",
    )(h, w_packed, w_small)


def _unit_lower_inverse(lows, eye, m8, moffs):
    each = lambda f, *lists: [f(*args) for args in zip(*lists)]
    l8 = each(lambda low: jnp.where(m8, low, 0.0), lows)
    l8_2 = each(_dot, l8, l8)
    l8_3 = each(_dot, l8, l8_2)
    l8_4 = each(_dot, l8_2, l8_2)
    p1 = each(lambda a, b, c: eye - a + b - c, l8, l8_2, l8_3)
    x = each(lambda p, l4: p + _dot(p, l4), p1, l8_4)
    for moff in moffs:
        xl = each(lambda xi, low: _dot(xi, jnp.where(moff, low, 0.0)), x, lows)
        x = each(lambda xi, xli: xi - _dot(xli, xi), x, xl)
    return x


def _delta_body(qkv_ref, z_ref, small_ref, convw_ref, alog_ref, dtb_ref, nw_ref, o_ref,
                halo_ref, y_ref, bt_ref, gt_ref, s_ref, u_ref, w_ref, qd_ref, qk_ref, kdt_ref, gl_ref):
    t = pl.program_id(0)
    nb, tm = qkv_ref.shape[0], qkv_ref.shape[1]
    assert tm & (tm - 1) == 0
    nsc = tm // SUPER
    halo = SUBLANES
    taps = CONV_WIDTH - 1

    @pl.when(t == 0)
    def _():
        halo_ref[...] = jnp.zeros(halo_ref.shape, F32)
        s_ref[...] = jnp.zeros(s_ref.shape, F32)

    srow = lax.broadcasted_iota(jnp.int32, (taps * tm, tm), 0)
    scol = lax.broadcasted_iota(jnp.int32, (taps * tm, tm), 1)
    tap = jnp.right_shift(srow, tm.bit_length() - 1)
    shifts = jnp.where((srow - tap * tm) - scol == tap + 1, 1.0, 0.0).astype(BF16)
    for b in range(nb):
        xb = qkv_ref[b]
        xf = xb.astype(F32)
        shifted = jnp.dot(shifts, xb, preferred_element_type=F32)
        acc = convw_ref[taps:taps + 1, :] * xf
        head = None
        for s in range(1, taps + 1):
            wrow = convw_ref[taps - s:taps - s + 1, :]
            acc = acc + wrow * shifted[(s - 1) * tm:s * tm, :]
            patch = wrow * halo_ref[b, halo - s:2 * halo - s, :]
            head = patch if head is None else head + patch
        y_ref[b] = _silu(acc)
        y_ref[b, 0:halo, :] = _silu(acc[0:halo, :] + head)
        halo_ref[b, 0:halo, :] = xf[tm - halo:tm, :]
        sm = small_ref[b]
        bt_ref[b] = _sigmoid(sm)
        gt_ref[b] = -jnp.exp(alog_ref[...]) * _softplus(sm + dtb_ref[...])

    row = lax.broadcasted_iota(jnp.int32, (SUPER, SUPER), 0)
    col = lax.broadcasted_iota(jnp.int32, (SUPER, SUPER), 1)
    blk = lambda idx, size: jnp.right_shift(idx, size.bit_length() - 1)
    same_chunk = blk(row, DN_CHUNK) == blk(col, DN_CHUNK)
    causal = same_chunk & (col <= row)
    strict = same_chunk & (col < row)
    causal_ones = jnp.where(causal, 1.0, 0.0).astype(BF16)
    eye = jnp.where(row == col, 1.0, 0.0).astype(F32)
    m8 = blk(row, 8) == blk(col, 8)
    moffs = [(blk(row, 2 * s) == blk(col, 2 * s)) & (blk(row, s) != blk(col, s)) for s in (8, 16, 32)]
    first_rows = row < DN_CHUNK
    first_cols = col < DN_CHUNK
    nw = nw_ref[...]

    each = lambda f, *lists: [f(*args) for args in zip(*lists)]
    hcols = lambda h: slice(h * HEAD_DIM, (h + 1) * HEAD_DIM)
    srows = lambda sc: slice(sc * SUPER, (sc + 1) * SUPER)
    unit = lambda b, sc, h: (b * nsc + sc) * N_HEADS + h
    units = [(b, sc, h) for b in range(nb) for sc in range(nsc) for h in range(N_HEADS)]
    kbs, ks, qs, decs, egcs = [], [], [], [], []
    for b in range(nb):
        for sc in range(nsc):
            rows = srows(sc)
            gc = _dot_exact_lhs(causal_ones, gt_ref[b, rows, :])
            gct = gc.T
            bt = bt_ref[b, rows, :]
            egc = jnp.exp(gc)
            gl0 = gc[DN_CHUNK - 1:DN_CHUNK, :]
            gl1 = gc[SUPER - 1:SUPER, :]
            ekd = jnp.exp(jnp.where(first_rows, gl0, gl1) - gc)
            egl = (jnp.exp(gl0), jnp.exp(gl1))
            for h in range(N_HEADS):
                lane = slice(ALPHA_LANE + h, ALPHA_LANE + h + 1)
                q = y_ref[b, rows, h * HEAD_DIM:(h + 1) * HEAD_DIM]
                k = y_ref[b, rows, WIDTH + h * HEAD_DIM:WIDTH + (h + 1) * HEAD_DIM]
                q = q * lax.rsqrt(jnp.sum(q * q, axis=-1, keepdims=True) + L2_EPS) * (HEAD_DIM ** -0.5)
                k = k * lax.rsqrt(jnp.sum(k * k, axis=-1, keepdims=True) + L2_EPS)
                dec = jnp.exp(jnp.where(causal, gc[:, lane] - gct[lane, :], -jnp.inf))
                kb = k * bt[:, BETA_LANE + h:BETA_LANE + h + 1]
                qd_ref[b, rows, hcols(h)] = (q * egc[:, lane]).astype(BF16)
                kdt_ref[unit(b, sc, h)] = (k * ekd[:, lane]).T.astype(BF16)
                for c in range(2):
                    gl_ref[unit(b, sc, h) * 2 + c] = jnp.broadcast_to(egl[c][:, lane], (SUBLANES, LANES))
                qs.append(q), ks.append(k), kbs.append(kb), decs.append(dec), egcs.append(egc[:, lane])
    lows = each(lambda kb, k, dec: jnp.where(strict, _dot_nt(kb, k) * dec, 0.0), kbs, ks, decs)
    qks = each(lambda q, k, dec: (_dot_nt(q, k) * dec).astype(BF16), qs, ks, decs)
    tmats = _unit_lower_inverse(lows, eye, m8, moffs)
    for (b, sc, h), tmat, kb, egc_col, qk in zip(units, tmats, kbs, egcs, qks):
        rows = srows(sc)
        v = y_ref[b, rows, 2 * WIDTH + h * HEAD_DIM:2 * WIDTH + (h + 1) * HEAD_DIM]
        u_ref[b, rows, hcols(h)] = _dot(tmat, v * bt_ref[b, rows, BETA_LANE + h:BETA_LANE + h + 1])
        w_ref[b, rows, hcols(h)] = _dot(tmat, kb * egc_col).astype(BF16)
        qk_ref[unit(b, sc, h)] = qk

    chains = [(b, h) for b in range(nb) for h in range(N_HEADS)]
    zero = jnp.zeros((HEAD_DIM, SUPER), BF16)
    for sc in range(nsc):
        rows = srows(sc)
        u = [u_ref[b, rows, hcols(h)] for b, h in chains]
        w = [w_ref[b, rows, hcols(h)] for b, h in chains]
        kdt = [kdt_ref[unit(b, sc, h)] for b, h in chains]
        kdt0 = [jnp.where(first_cols, x, zero) for x in kdt]
        kdt1 = [jnp.where(first_cols, zero, x) for x in kdt]
        g0 = [jnp.tile(gl_ref[unit(b, sc, h) * 2], (HEAD_DIM // SUBLANES, 1)) for b, h in chains]
        g1 = [jnp.tile(gl_ref[unit(b, sc, h) * 2 + 1], (HEAD_DIM // SUBLANES, 1)) for b, h in chains]
        s0 = [s_ref[b * N_HEADS + h] for b, h in chains]
        vn0 = each(lambda ui, wi, si: ui - _dot(wi, si), u, w, s0)
        s1 = each(lambda si, gi, ki, vi: si * gi + _dot(ki, vi), s0, g0, kdt0, vn0)
        vn1 = each(lambda ui, wi, si: ui - _dot(wi, si), u, w, s1)
        s2 = each(lambda si, gi, ki, vi: si * gi + _dot(ki, vi), s1, g1, kdt1, vn1)
        for ci, (b, h) in enumerate(chains):
            s_ref[b * N_HEADS + h] = s2[ci]
        qd = [qd_ref[b, rows, hcols(h)] for b, h in chains]
        o0 = each(_dot, qd, s0)
        o1 = each(_dot, qd, s1)
        vn = each(lambda a, c: jnp.where(first_rows, a, c), vn0, vn1)
        oq = each(lambda qki, vi: _dot(qki, vi), [qk_ref[unit(b, sc, h)] for b, h in chains], vn)
        for ci, (b, h) in enumerate(chains):
            o = jnp.where(first_rows, o0[ci], o1[ci]) + oq[ci]
            o = o * lax.rsqrt(jnp.mean(o * o, axis=-1, keepdims=True) + NORM_EPS) * nw
            o_ref[b, rows, hcols(h)] = (o * _silu(z_ref[b, rows, hcols(h)].astype(F32))).astype(o_ref.dtype)


def _delta_branch(proj, small, conv_w, a_log, dt_bias, norm_w, batch, seq, tm):
    n = proj.shape[0]
    nt = seq // tm
    units = batch * (tm // SUPER) * N_HEADS
    lane_row = lambda vals, off: jnp.zeros((1, LANES), F32).at[0, off:off + N_HEADS].set(vals)
    tok = lambda width, col: pl.BlockSpec((batch, tm, width), lambda t: (0, t, col))
    full = lambda shape: pl.BlockSpec(shape, lambda t: (0,) * len(shape))
    proj3 = proj.reshape(batch, seq, proj.shape[1])
    out = pl.pallas_call(
        _delta_body,
        grid=(nt,),
        in_specs=[
            tok(3 * WIDTH, DN_OFF // (3 * WIDTH)),
            tok(WIDTH, (DN_OFF + 3 * WIDTH) // WIDTH),
            tok(LANES, 0),
            full((CONV_WIDTH, 3 * WIDTH)),
            full((1, LANES)),
            full((1, LANES)),
            full((1, HEAD_DIM)),
        ],
        out_specs=tok(WIDTH, 0),
        out_shape=jax.ShapeDtypeStruct((batch, seq, WIDTH), BF16),
        scratch_shapes=[
            pltpu.VMEM((batch, 2 * SUBLANES, 3 * WIDTH), F32),
            pltpu.VMEM((batch, tm, 3 * WIDTH), F32),
            pltpu.VMEM((batch, tm, LANES), F32),
            pltpu.VMEM((batch, tm, LANES), F32),
            pltpu.VMEM((batch * N_HEADS, HEAD_DIM, HEAD_DIM), F32),
            pltpu.VMEM((batch, tm, WIDTH), F32),
            pltpu.VMEM((batch, tm, WIDTH), BF16),
            pltpu.VMEM((batch, tm, WIDTH), BF16),
            pltpu.VMEM((units, SUPER, SUPER), BF16),
            pltpu.VMEM((units, HEAD_DIM, SUPER), BF16),
            pltpu.VMEM((2 * units, SUBLANES, LANES), F32),
        ],
        compiler_params=pltpu.CompilerParams(
            dimension_semantics=("arbitrary",), vmem_limit_bytes=VMEM_LIMIT),
        name="delta",
    )(proj3, proj3, small.reshape(batch, seq, LANES), conv_w,
      lane_row(a_log, ALPHA_LANE), lane_row(dt_bias, ALPHA_LANE), norm_w.reshape(1, HEAD_DIM))
    return out.reshape(n, WIDTH)


def _sg_tile(u_ref, v_ref, z_ref, lnw_ref, lnb_ref, ws_ref, bs_ref):
    tm = u_ref.shape[0]
    v = v_ref[...].astype(F32)
    mu = jnp.mean(v, axis=-1, keepdims=True)
    vc = v - mu
    var = jnp.mean(vc * vc, axis=-1, keepdims=True)
    vn = (vc * lax.rsqrt(var + LN_EPS) * lnw_ref[...] + lnb_ref[...]).astype(BF16)
    row = lax.broadcasted_iota(jnp.int32, (SG_CHUNK, SG_CHUNK), 0)
    col = lax.broadcasted_iota(jnp.int32, (SG_CHUNK, SG_CHUNK), 1)
    tril = col <= row
    gate = u_ref[...].astype(F32) * _silu(z_ref[...].astype(F32))
    bs = bs_ref[...]
    cols = []
    for g in range(N_HEADS):
        wg = jnp.where(tril, ws_ref[g], 0.0).astype(BF16)
        bg = bs[:, g:g + 1]
        cs = slice(g * HEAD_DIM, (g + 1) * HEAD_DIM)
        chunks = []
        for c in range(tm // SG_CHUNK):
            rs = slice(c * SG_CHUNK, (c + 1) * SG_CHUNK)
            mixed = jnp.dot(wg, vn[rs, cs], preferred_element_type=F32) + bg
            chunks.append((gate[rs, cs] * mixed).astype(BF16))
        cols.append(jnp.concatenate(chunks, axis=0))
    return jnp.concatenate(cols, axis=1)


LOG2E = 1.4426950408889634
ATTN_SUB = 256


def _fox_prep_body(q_ref, k_ref, v_ref, small_ref, fb_ref, qw_ref, kw_ref,
                   qn_ref, kn_ref, vt_ref, ccol_ref, crow_ref, carry_ref):
    t = pl.program_id(1)
    tm = q_ref.shape[0]

    @pl.when(t == 0)
    def _():
        carry_ref[...] = jnp.zeros(carry_ref.shape, F32)

    for h in range(N_HEADS):
        cs = slice(h * HEAD_DIM, (h + 1) * HEAD_DIM)
        q = q_ref[:, cs].astype(F32)
        k = k_ref[:, cs].astype(F32)
        q = q * lax.rsqrt(jnp.mean(q * q, axis=-1, keepdims=True) + NORM_EPS) * qw_ref[...]
        k = k * lax.rsqrt(jnp.mean(k * k, axis=-1, keepdims=True) + NORM_EPS) * kw_ref[...]
        qn_ref[:, cs] = (q * (HEAD_DIM ** -0.5 * LOG2E)).astype(BF16)
        kn_ref[:, cs] = k.astype(BF16)
        vt_ref[0, h, 0] = v_ref[:, cs].astype(F32).T.astype(BF16)

    logf = _log_sigmoid(small_ref[...] + fb_ref[...])
    row = lax.broadcasted_iota(jnp.int32, (tm, tm), 0)
    col = lax.broadcasted_iota(jnp.int32, (tm, tm), 1)
    c = _dot_exact_lhs(jnp.where(col <= row, 1.0, 0.0), logf) + carry_ref[0:1, :]
    carry_ref[0:1, :] = c[tm - 1:tm, :]
    c2 = c * LOG2E
    ct = c2.T
    for h in range(N_HEADS):
        lane = FORGET_LANE + h
        ccol_ref[:, h * HEAD_DIM:(h + 1) * HEAD_DIM] = jnp.broadcast_to(c2[:, lane:lane + 1], (tm, HEAD_DIM))
        crow_ref[h * SUBLANES:(h + 1) * SUBLANES, :] = jnp.broadcast_to(ct[lane:lane + 1, :], (SUBLANES, tm))


def _fox_prep(proj, small, f_bias, qn_w, kn_w, batch, seq, tm):
    n = proj.shape[0]
    nt = seq // tm
    tok = lambda width, col: pl.BlockSpec((tm, width), lambda b, t: (b * nt + t, col))
    full = lambda shape: pl.BlockSpec(shape, lambda b, t: (0,) * len(shape))
    fb = jnp.zeros((1, LANES), F32).at[0, FORGET_LANE:FORGET_LANE + N_HEADS].set(f_bias)
    out_tok = pl.BlockSpec((tm, WIDTH), lambda b, t: (b * nt + t, 0))
    return pl.pallas_call(
        _fox_prep_body,
        grid=(batch, nt),
        in_specs=[
            tok(WIDTH, FOX_OFF // WIDTH), tok(WIDTH, FOX_OFF // WIDTH + 1), tok(WIDTH, FOX_OFF // WIDTH + 2),
            tok(LANES, 0),
            full((1, LANES)), full((1, HEAD_DIM)), full((1, HEAD_DIM)),
        ],
        out_specs=[out_tok, out_tok,
                   pl.BlockSpec((1, N_HEADS, 1, HEAD_DIM, tm), lambda b, t: (b, 0, t, 0, 0)),
                   out_tok,
                   pl.BlockSpec((N_HEADS * SUBLANES, tm), lambda b, t: (b, t))],
        out_shape=[jax.ShapeDtypeStruct((n, WIDTH), BF16)] * 2
        + [jax.ShapeDtypeStruct((batch, N_HEADS, nt, HEAD_DIM, tm), BF16),
           jax.ShapeDtypeStruct((n, WIDTH), F32),
           jax.ShapeDtypeStruct((batch * N_HEADS * SUBLANES, seq), F32)],
        scratch_shapes=[pltpu.VMEM((SUBLANES, LANES), F32)],
        compiler_params=pltpu.CompilerParams(
            dimension_semantics=("parallel", "arbitrary"), vmem_limit_bytes=VMEM_LIMIT),
        name="fox_prep",
    )(proj, proj, proj, small, fb, qn_w.reshape(1, HEAD_DIM), kn_w.reshape(1, HEAD_DIM))


def _fox_attn_body(q_ref, k_ref, vt_ref, cq_ref, ck_ref, z_ref, o_ref,
                   m_ref, l_ref, acc_ref, sta_ref, stb_ref):
    i = pl.program_id(2)
    blk = q_ref.shape[0]
    m_ref[...] = jnp.full(m_ref.shape, -jnp.inf, F32)
    l_ref[...] = jnp.zeros(l_ref.shape, F32)
    acc_ref[...] = jnp.zeros(acc_ref.shape, F32)
    subs = [slice(s * ATTN_SUB, (s + 1) * ATTN_SUB) for s in range(blk // ATTN_SUB)]
    each = lambda f, *lists: [f(*args) for args in zip(*lists)]

    def scores(j, st_ref, diagonal=False):
        for si, s in enumerate(subs):
            nk = s.stop if diagonal else blk
            rows = pl.ds(pl.multiple_of(j * blk, blk), nk)
            ck = jnp.tile(ck_ref[rows, :], (1, ATTN_SUB // HEAD_DIM))
            st_ref[si, 0:nk, :] = lax.dot_general(k_ref[rows, :], q_ref[s, :], (((1,), (1,)), ((), ())),
                                                  preferred_element_type=F32) - ck

    def softmax_pv(j, st_ref, diagonal=False):
        nks = [s.stop if diagonal else blk for s in subs]
        st = [st_ref[si, 0:nk, :] for si, nk in enumerate(nks)]
        cq = [cq_ref[0:1, s] for s in subs]
        if diagonal:
            def causal(x, s, nk):
                kpos = lax.broadcasted_iota(jnp.int32, (nk, ATTN_SUB), 0)
                qpos = lax.broadcasted_iota(jnp.int32, (nk, ATTN_SUB), 1)
                return jnp.where(kpos <= qpos + s.start, x, -jnp.inf)
            st = each(causal, st, subs, nks)
        m_prev = [m_ref[0:1, s] for s in subs]
        m_new = each(lambda mp, x, c: jnp.maximum(mp, jnp.max(x, axis=0, keepdims=True) + c), m_prev, st, cq)
        alpha = each(lambda mp, mn: jnp.exp2(mp - mn), m_prev, m_new)
        p = each(lambda x, c, mn: jnp.exp2(x + (c - mn)), st, cq, m_new)
        for s, nk, a, pi, mn in zip(subs, nks, alpha, p, m_new):
            l_ref[0:1, s] = a * l_ref[0:1, s] + jnp.sum(pi, axis=0, keepdims=True)
            acc_ref[:, s] = a * acc_ref[:, s] + jnp.dot(vt_ref[0, 0, j, :, 0:nk], pi.astype(BF16),
                                                        preferred_element_type=F32)
            m_ref[0:1, s] = mn

    def step(j, cur_ref, nxt_ref, next_diagonal=False):
        scores(j + 1, nxt_ref, next_diagonal)
        softmax_pv(j, cur_ref)

    def pair(p, carry):
        step(2 * p, sta_ref, stb_ref)
        step(2 * p + 1, stb_ref, sta_ref)
        return carry

    @pl.when(i == 0)
    def _():
        scores(0, sta_ref, True)
        softmax_pv(0, sta_ref, True)

    @pl.when(i > 0)
    def _():
        scores(0, sta_ref)
        lax.fori_loop(0, (i - 1) // 2, pair, 0)

    @pl.when((i > 0) & (i % 2 == 1))
    def _():
        step(i - 1, sta_ref, stb_ref, True)
        softmax_pv(i, stb_ref, True)

    @pl.when((i > 0) & (i % 2 == 0))
    def _():
        step(i - 2, sta_ref, stb_ref)
        step(i - 1, stb_ref, sta_ref, True)
        softmax_pv(i, sta_ref, True)

    o = (acc_ref[...] / l_ref[0:1, :]).T
    o_ref[...] = (o * _silu(z_ref[...].astype(F32))).astype(o_ref.dtype)


def _fox_attention(qn, kn, vt, ccol, crow, proj, batch, seq, blk):
    n = qn.shape[0]
    nb = seq // blk
    assert vt.shape == (batch, N_HEADS, nb, HEAD_DIM, blk)
    qspec = lambda arr_col: pl.BlockSpec((blk, HEAD_DIM), lambda b, h, i: (b * nb + i, arr_col(h)))
    seq_spec = pl.BlockSpec((seq, HEAD_DIM), lambda b, h, i: (b, h))
    return pl.pallas_call(
        _fox_attn_body,
        grid=(batch, N_HEADS, nb),
        in_specs=[
            qspec(lambda h: h), seq_spec,
            pl.BlockSpec((1, 1, nb, HEAD_DIM, blk), lambda b, h, i: (b, h, 0, 0, 0)),
            pl.BlockSpec((SUBLANES, blk), lambda b, h, i: (b * N_HEADS + h, i)),
            seq_spec,
            qspec(lambda h: (FOX_OFF + 3 * WIDTH) // HEAD_DIM + h),
        ],
        out_specs=qspec(lambda h: h),
        out_shape=jax.ShapeDtypeStruct((n, WIDTH), BF16),
        scratch_shapes=[pltpu.VMEM((SUBLANES, blk), F32), pltpu.VMEM((SUBLANES, blk), F32),
                        pltpu.VMEM((HEAD_DIM, blk), F32),
                        pltpu.VMEM((blk // ATTN_SUB, blk, ATTN_SUB), F32),
                        pltpu.VMEM((blk // ATTN_SUB, blk, ATTN_SUB), F32)],
        compiler_params=pltpu.CompilerParams(
            dimension_semantics=("parallel", "parallel", "arbitrary"),
            vmem_limit_bytes=VMEM_LIMIT),
        name="fox_attention",
    )(qn, kn, vt, crow, ccol, proj)


def _merge_body(od_ref, of_ref, g_ref, x_ref, wb_ref, wo_ref, nw_ref,
                u_ref, v_ref, z_ref, lnw_ref, lnb_ref, ws_ref, bs_ref, *out_refs):
    branches = (od_ref[...], _sg_tile(u_ref, v_ref, z_ref, lnw_ref, lnb_ref, ws_ref, bs_ref), of_ref[...])
    merged = None
    for nbr, branch in enumerate(branches):
        up = jnp.dot(branch, wb_ref[nbr], preferred_element_type=F32)
        term = _sigmoid(g_ref[:, nbr * D_MODEL:(nbr + 1) * D_MODEL].astype(F32)) * up
        merged = term if merged is None else merged + term
    x_new = x_ref[...] + jnp.dot(merged.astype(BF16), wo_ref[...], preferred_element_type=F32)
    normed_ref = out_refs[-1]
    normed_ref[...] = _rms_norm(x_new, nw_ref[...]).astype(normed_ref.dtype)
    if len(out_refs) == 2:
        out_refs[0][...] = x_new


def _merge(o_delta, o_fox, proj, x2d, w_branch, w_out, next_norm_w, sg_ln_w, sg_ln_b, w_s, b_s,
           layer, last, tm):
    n = x2d.shape[0]
    assert tm % SG_CHUNK == 0
    tok = lambda width, col: pl.BlockSpec((tm, width), lambda i: (i, col))
    full = lambda shape: pl.BlockSpec(shape, lambda i: (0,) * len(shape))
    bs_t = jnp.zeros((SG_CHUNK, LANES), F32).at[:, :N_HEADS].set(b_s.T)
    if last:
        out_specs = [tok(D_MODEL, 0)]
        out_shape = [jax.ShapeDtypeStruct((n, D_MODEL), F32)]
    else:
        out_specs = [tok(D_MODEL, 0), tok(D_MODEL, 0)]
        out_shape = [jax.ShapeDtypeStruct((n, D_MODEL), F32), jax.ShapeDtypeStruct((n, D_MODEL), BF16)]
    return pl.pallas_call(
        _merge_body,
        grid=(n // tm,),
        in_specs=[
            tok(WIDTH, 0), tok(WIDTH, 0),
            tok(N_BRANCHES * D_MODEL, GATES_OFF // (N_BRANCHES * D_MODEL)),
            tok(D_MODEL, 0),
            pl.BlockSpec((None, N_BRANCHES, WIDTH, D_MODEL), lambda i: (layer, 0, 0, 0)),
            pl.BlockSpec((None, D_MODEL, D_MODEL), lambda i: (layer, 0, 0)),
            full((1, D_MODEL)),
            tok(WIDTH, SG_OFF // WIDTH), tok(WIDTH, SG_OFF // WIDTH + 1), tok(WIDTH, SG_OFF // WIDTH + 2),
            full((1, WIDTH)), full((1, WIDTH)),
            full((N_HEADS, SG_CHUNK, SG_CHUNK)), full((SG_CHUNK, LANES)),
        ],
        out_specs=out_specs,
        out_shape=out_shape,
        compiler_params=pltpu.CompilerParams(
            dimension_semantics=("parallel",), vmem_limit_bytes=VMEM_LIMIT),
        name="merge",
    )(o_delta, o_fox, proj, x2d, w_branch, w_out, next_norm_w.reshape(1, D_MODEL),
      proj, proj, proj, sg_ln_w.reshape(1, WIDTH), sg_ln_b.reshape(1, WIDTH), w_s, bs_t)


def _first_norm_body(x_ref, w_ref, o_ref):
    o_ref[...] = _rms_norm(x_ref[...], w_ref[...]).astype(o_ref.dtype)


def _first_norm(x2d, w, tm):
    n = x2d.shape[0]
    return pl.pallas_call(
        _first_norm_body,
        grid=(n // tm,),
        in_specs=[pl.BlockSpec((tm, D_MODEL), lambda i: (i, 0)),
                  pl.BlockSpec((1, D_MODEL), lambda i: (0, 0))],
        out_specs=pl.BlockSpec((tm, D_MODEL), lambda i: (i, 0)),
        out_shape=jax.ShapeDtypeStruct((n, D_MODEL), BF16),
        compiler_params=pltpu.CompilerParams(dimension_semantics=("parallel",)),
        name="first_norm",
    )(x2d, w.reshape(1, D_MODEL))


_IN_SIZES = (3 * WIDTH, WIDTH, N_HEADS, N_HEADS, 2 * WIDTH, WIDTH, 3 * WIDTH, WIDTH, N_HEADS,
             N_BRANCHES * D_MODEL)
_IN_STARTS = tuple(sum(_IN_SIZES[:i]) for i in range(len(_IN_SIZES) + 1))
D_IN = _IN_STARTS[-1]
_PACK_RUNS = (
    (_IN_STARTS[9], _IN_SIZES[9], GATES_OFF),
    (_IN_STARTS[0], _IN_SIZES[0] + _IN_SIZES[1], DN_OFF),
    (_IN_STARTS[4], _IN_SIZES[4] + _IN_SIZES[5], SG_OFF),
    (_IN_STARTS[6], _IN_SIZES[6] + _IN_SIZES[7], FOX_OFF),
)
_SMALL_RUNS = (
    (_IN_STARTS[2], _IN_SIZES[2] + _IN_SIZES[3], BETA_LANE),
    (_IN_STARTS[8], _IN_SIZES[8], FORGET_LANE),
)


def _pack_body(w_ref, o_ref, small_ref):
    rows = w_ref.shape[1]
    for src, length, dst in _PACK_RUNS:
        o_ref[0, :, dst:dst + length] = w_ref[0, :, src:src + length].astype(BF16)
    small_ref[0] = jnp.zeros((rows, LANES), BF16)
    for src, length, dst in _SMALL_RUNS:
        small_ref[0, :, dst:dst + length] = w_ref[0, :, src:src + length].astype(BF16)


def _pack_w_in(w_in, rows):
    depth, d_model, d_in = w_in.shape
    assert d_in == D_IN and d_model % rows == 0
    return pl.pallas_call(
        _pack_body,
        grid=(depth, d_model // rows),
        in_specs=[pl.BlockSpec((1, rows, D_IN), lambda l, i: (l, i, 0))],
        out_specs=[pl.BlockSpec((1, rows, PACKED_COLS), lambda l, i: (l, i, 0)),
                   pl.BlockSpec((1, rows, LANES), lambda l, i: (l, i, 0))],
        out_shape=[jax.ShapeDtypeStruct((depth, d_model, PACKED_COLS), BF16),
                   jax.ShapeDtypeStruct((depth, d_model, LANES), BF16)],
        compiler_params=pltpu.CompilerParams(
            dimension_semantics=("parallel", "parallel"), vmem_limit_bytes=VMEM_LIMIT),
        name="pack_w_in",
    )(w_in)


def _tile(total, want):
    t = min(total, want)
    assert total % t == 0, (total, t)
    return t


def kernel(x, norm_w, w_in, f_bias, conv_w, a_log, dt_bias, dn_norm_w, sg_ln_w, sg_ln_b,
           w_spatial, b_spatial, fox_qnorm_w, fox_knorm_w, w_branch, w_out, final_norm_w):
    batch, seq, d_model = x.shape
    assert d_model == D_MODEL and seq % SUPER == 0
    n = batch * seq
    depth = w_in.shape[0]
    x2d = x.reshape(n, D_MODEL)
    tm_proj = _tile(n, 2048)
    tm_delta = _tile(seq, SUPER)
    tm_prep = _tile(seq, 1024)
    blk_attn = _tile(seq, 1024)
    tm_merge = _tile(n, 512)
    w_packed, w_small = _pack_w_in(w_in, _tile(D_MODEL, 128))
    w_branch = w_branch.astype(BF16)
    w_out = w_out.astype(BF16)
    h = _first_norm(x2d, norm_w[0], _tile(n, 1024))
    for l in range(depth):
        last = l == depth - 1
        proj, small = _inproj(h, w_packed, w_small, l, tm_proj)
        o_delta = _delta_branch(proj, small, conv_w[l], a_log[l], dt_bias[l], dn_norm_w[l],
                                batch, seq, tm_delta)
        qn, kn, vt, ccol, crow = _fox_prep(proj, small, f_bias[l], fox_qnorm_w[l], fox_knorm_w[l],
                                           batch, seq, tm_prep)
        o_fox = _fox_attention(qn, kn, vt, ccol, crow, proj, batch, seq, blk_attn)
        outs = _merge(o_delta, o_fox, proj, x2d, w_branch, w_out, final_norm_w if last else norm_w[l + 1],
                      sg_ln_w[l], sg_ln_b[l], w_spatial[l], b_spatial[l], l, last, tm_merge)
        if last:
            return outs[0].reshape(batch, seq, D_MODEL)
        x2d, h = outs
```

```python
import jax
import jax.numpy as jnp
from jax import lax
from jax.experimental import pallas as pl
from jax.experimental.pallas import tpu as pltpu

F32 = jnp.float32
BF16 = jnp.bfloat16

D_MODEL = 1024
HEAD_DIM = 128
N_HEADS = 4
WIDTH = N_HEADS * HEAD_DIM
N_BRANCHES = 3
DN_CHUNK = 64
SUPER = 2 * DN_CHUNK
CONV_WIDTH = 4
SG_CHUNK = 128
NORM_EPS = 1e-6
LN_EPS = 1e-5
L2_EPS = 1e-6
LANES = 128
SUBLANES = 8

GATES_OFF = 0
DN_OFF = GATES_OFF + N_BRANCHES * D_MODEL
SG_OFF = DN_OFF + 4 * WIDTH
FOX_OFF = SG_OFF + 3 * WIDTH
PACKED_COLS = FOX_OFF + 4 * WIDTH
PROJ_TILE_N = PACKED_COLS // 4
assert PROJ_TILE_N % LANES == 0
BETA_LANE, ALPHA_LANE, FORGET_LANE = 0, N_HEADS, 2 * N_HEADS

VMEM_LIMIT = 48 * 1024 * 1024


def _dot(a, b):
    return jnp.dot(a.astype(BF16), b.astype(BF16), preferred_element_type=F32)


def _dot_nt(a, b):
    return lax.dot_general(a.astype(BF16), b.astype(BF16), (((1,), (1,)), ((), ())),
                           preferred_element_type=F32)


def _dot_exact_lhs(a01, x):
    a = a01.astype(BF16)
    x1 = x.astype(BF16)
    r1 = x - x1.astype(F32)
    x2 = r1.astype(BF16)
    x3 = (r1 - x2.astype(F32)).astype(BF16)
    d = lambda p: jnp.dot(a, p, preferred_element_type=F32)
    return d(x1) + d(x2) + d(x3)


def _sigmoid(x):
    return 1.0 / (1.0 + jnp.exp(-x))


def _silu(x):
    return x * _sigmoid(x)


def _softplus(x):
    return jnp.maximum(x, 0.0) + jnp.log(1.0 + jnp.exp(-jnp.abs(x)))


def _log_sigmoid(x):
    return -_softplus(-x)


def _rms_norm(x, w):
    return x * lax.rsqrt(jnp.mean(x * x, axis=-1, keepdims=True) + NORM_EPS) * w


def _inproj_body(h_ref, w_ref, ws_ref, o_ref, small_ref):
    @pl.when(pl.program_id(1) == 0)
    def _():
        small_ref[...] = jnp.dot(h_ref[...], ws_ref[...], preferred_element_type=F32)

    o_ref[...] = jnp.dot(h_ref[...], w_ref[...], preferred_element_type=F32).astype(o_ref.dtype)


def _inproj(h, w_packed, w_small, layer, tm):
    n = h.shape[0]
    return pl.pallas_call(
        _inproj_body,
        grid=(n // tm, PACKED_COLS // PROJ_TILE_N),
        in_specs=[
            pl.BlockSpec((tm, D_MODEL), lambda i, j: (i, 0)),
            pl.BlockSpec((None, D_MODEL, PROJ_TILE_N), lambda i, j: (layer, 0, j)),
            pl.BlockSpec((None, D_MODEL, LANES), lambda i, j: (layer, 0, 0)),
        ],
        out_specs=[pl.BlockSpec((tm, PROJ_TILE_N), lambda i, j: (i, j)),
                   pl.BlockSpec((tm, LANES), lambda i, j: (i, 0))],
        out_shape=[jax.ShapeDtypeStruct((n, PACKED_COLS), BF16),
                   jax.ShapeDtypeStruct((n, LANES), F32)],
        compiler_params=pltpu.CompilerParams(
            dimension_semantics=("parallel", "arbitrary"), vmem_limit_bytes=VMEM_LIMIT),
        name="inproj",
    )(h, w_packed, w_small)


def _unit_lower_inverse(lows, eye, m8, moffs):
    each = lambda f, *lists: [f(*args) for args in zip(*lists)]
    l8 = each(lambda low: jnp.where(m8, low, 0.0), lows)
    l8_2 = each(_dot, l8, l8)
    l8_3 = each(_dot, l8, l8_2)
    l8_4 = each(_dot, l8_2, l8_2)
    p1 = each(lambda a, b, c: eye - a + b - c, l8, l8_2, l8_3)
    x = each(lambda p, l4: p + _dot(p, l4), p1, l8_4)
    for moff in moffs:
        xl = each(lambda xi, low: _dot(xi, jnp.where(moff, low, 0.0)), x, lows)
        x = each(lambda xi, xli: xi - _dot(xli, xi), x, xl)
    return x


def _delta_body(qkv_ref, z_ref, small_ref, convw_ref, alog_ref, dtb_ref, nw_ref, o_ref,
                halo_ref, y_ref, bt_ref, gt_ref, s_ref, u_ref, w_ref, qd_ref, qk_ref, kdt_ref, gl_ref):
    t = pl.program_id(0)
    nb, tm = qkv_ref.shape[0], qkv_ref.shape[1]
    assert tm & (tm - 1) == 0
    nsc = tm // SUPER
    halo = SUBLANES
    taps = CONV_WIDTH - 1

    @pl.when(t == 0)
    def _():
        halo_ref[...] = jnp.zeros(halo_ref.shape, F32)
        s_ref[...] = jnp.zeros(s_ref.shape, F32)

    srow = lax.broadcasted_iota(jnp.int32, (taps * tm, tm), 0)
    scol = lax.broadcasted_iota(jnp.int32, (taps * tm, tm), 1)
    tap = jnp.right_shift(srow, tm.bit_length() - 1)
    shifts = jnp.where((srow - tap * tm) - scol == tap + 1, 1.0, 0.0).astype(BF16)
    for b in range(nb):
        xb = qkv_ref[b]
        xf = xb.astype(F32)
        shifted = jnp.dot(shifts, xb, preferred_element_type=F32)
        acc = convw_ref[taps:taps + 1, :] * xf
        head = None
        for s in range(1, taps + 1):
            wrow = convw_ref[taps - s:taps - s + 1, :]
            acc = acc + wrow * shifted[(s - 1) * tm:s * tm, :]
            patch = wrow * halo_ref[b, halo - s:2 * halo - s, :]
            head = patch if head is None else head + patch
        y_ref[b] = _silu(acc)
        y_ref[b, 0:halo, :] = _silu(acc[0:halo, :] + head)
        halo_ref[b, 0:halo, :] = xf[tm - halo:tm, :]
        sm = small_ref[b]
        bt_ref[b] = _sigmoid(sm)
        gt_ref[b] = -jnp.exp(alog_ref[...]) * _softplus(sm + dtb_ref[...])

    row = lax.broadcasted_iota(jnp.int32, (SUPER, SUPER), 0)
    col = lax.broadcasted_iota(jnp.int32, (SUPER, SUPER), 1)
    blk = lambda idx, size: jnp.right_shift(idx, size.bit_length() - 1)
    same_chunk = blk(row, DN_CHUNK) == blk(col, DN_CHUNK)
    causal = same_chunk & (col <= row)
    strict = same_chunk & (col < row)
    causal_ones = jnp.where(causal, 1.0, 0.0).astype(BF16)
    eye = jnp.where(row == col, 1.0, 0.0).astype(F32)
    m8 = blk(row, 8) == blk(col, 8)
    moffs = [(blk(row, 2 * s) == blk(col, 2 * s)) & (blk(row, s) != blk(col, s)) for s in (8, 16, 32)]
    first_rows = row < DN_CHUNK
    first_cols = col < DN_CHUNK
    nw = nw_ref[...]

    each = lambda f, *lists: [f(*args) for args in zip(*lists)]
    hcols = lambda h: slice(h * HEAD_DIM, (h + 1) * HEAD_DIM)
    srows = lambda sc: slice(sc * SUPER, (sc + 1) * SUPER)
    unit = lambda b, sc, h: (b * nsc + sc) * N_HEADS + h
    units = [(b, sc, h) for b in range(nb) for sc in range(nsc) for h in range(N_HEADS)]
    kbs, ks, qs, decs, egcs = [], [], [], [], []
    for b in range(nb):
        for sc in range(nsc):
            rows = srows(sc)
            gc = _dot_exact_lhs(causal_ones, gt_ref[b, rows, :])
            gct = gc.T
            bt = bt_ref[b, rows, :]
            egc = jnp.exp(gc)
            gl0 = gc[DN_CHUNK - 1:DN_CHUNK, :]
            gl1 = gc[SUPER - 1:SUPER, :]
            ekd = jnp.exp(jnp.where(first_rows, gl0, gl1) - gc)
            egl = (jnp.exp(gl0), jnp.exp(gl1))
            for h in range(N_HEADS):
                lane = slice(ALPHA_LANE + h, ALPHA_LANE + h + 1)
                q = y_ref[b, rows, h * HEAD_DIM:(h + 1) * HEAD_DIM]
                k = y_ref[b, rows, WIDTH + h * HEAD_DIM:WIDTH + (h + 1) * HEAD_DIM]
                q = q * lax.rsqrt(jnp.sum(q * q, axis=-1, keepdims=True) + L2_EPS) * (HEAD_DIM ** -0.5)
                k = k * lax.rsqrt(jnp.sum(k * k, axis=-1, keepdims=True) + L2_EPS)
                dec = jnp.exp(jnp.where(causal, gc[:, lane] - gct[lane, :], -jnp.inf))
                kb = k * bt[:, BETA_LANE + h:BETA_LANE + h + 1]
                qd_ref[b, rows, hcols(h)] = (q * egc[:, lane]).astype(BF16)
                kdt_ref[unit(b, sc, h)] = (k * ekd[:, lane]).T.astype(BF16)
                for c in range(2):
                    gl_ref[unit(b, sc, h) * 2 + c] = jnp.broadcast_to(egl[c][:, lane], (SUBLANES, LANES))
                qs.append(q), ks.append(k), kbs.append(kb), decs.append(dec), egcs.append(egc[:, lane])
    lows = each(lambda kb, k, dec: jnp.where(strict, _dot_nt(kb, k) * dec, 0.0), kbs, ks, decs)
    qks = each(lambda q, k, dec: (_dot_nt(q, k) * dec).astype(BF16), qs, ks, decs)
    tmats = _unit_lower_inverse(lows, eye, m8, moffs)
    for (b, sc, h), tmat, kb, egc_col, qk in zip(units, tmats, kbs, egcs, qks):
        rows = srows(sc)
        v = y_ref[b, rows, 2 * WIDTH + h * HEAD_DIM:2 * WIDTH + (h + 1) * HEAD_DIM]
        u_ref[b, rows, hcols(h)] = _dot(tmat, v * bt_ref[b, rows, BETA_LANE + h:BETA_LANE + h + 1])
        w_ref[b, rows, hcols(h)] = _dot(tmat, kb * egc_col).astype(BF16)
        qk_ref[unit(b, sc, h)] = qk

    chains = [(b, h) for b in range(nb) for h in range(N_HEADS)]
    zero = jnp.zeros((HEAD_DIM, SUPER), BF16)
    for sc in range(nsc):
        rows = srows(sc)
        u = [u_ref[b, rows, hcols(h)] for b, h in chains]
        w = [w_ref[b, rows, hcols(h)] for b, h in chains]
        kdt = [kdt_ref[unit(b, sc, h)] for b, h in chains]
        kdt0 = [jnp.where(first_cols, x, zero) for x in kdt]
        kdt1 = [jnp.where(first_cols, zero, x) for x in kdt]
        g0 = [jnp.tile(gl_ref[unit(b, sc, h) * 2], (HEAD_DIM // SUBLANES, 1)) for b, h in chains]
        g1 = [jnp.tile(gl_ref[unit(b, sc, h) * 2 + 1], (HEAD_DIM // SUBLANES, 1)) for b, h in chains]
        s0 = [s_ref[b * N_HEADS + h] for b, h in chains]
        vn0 = each(lambda ui, wi, si: ui - _dot(wi, si), u, w, s0)
        s1 = each(lambda si, gi, ki, vi: si * gi + _dot(ki, vi), s0, g0, kdt0, vn0)
        vn1 = each(lambda ui, wi, si: ui - _dot(wi, si), u, w, s1)
        s2 = each(lambda si, gi, ki, vi: si * gi + _dot(ki, vi), s1, g1, kdt1, vn1)
        for ci, (b, h) in enumerate(chains):
            s_ref[b * N_HEADS + h] = s2[ci]
        qd = [qd_ref[b, rows, hcols(h)] for b, h in chains]
        o0 = each(_dot, qd, s0)
        o1 = each(_dot, qd, s1)
        vn = each(lambda a, c: jnp.where(first_rows, a, c), vn0, vn1)
        oq = each(lambda qki, vi: _dot(qki, vi), [qk_ref[unit(b, sc, h)] for b, h in chains], vn)
        for ci, (b, h) in enumerate(chains):
            o = jnp.where(first_rows, o0[ci], o1[ci]) + oq[ci]
            o = o * lax.rsqrt(jnp.mean(o * o, axis=-1, keepdims=True) + NORM_EPS) * nw
            o_ref[b, rows, hcols(h)] = (o * _silu(z_ref[b, rows, hcols(h)].astype(F32))).astype(o_ref.dtype)


def _delta_branch(proj, small, conv_w, a_log, dt_bias, norm_w, batch, seq, tm):
    n = proj.shape[0]
    nt = seq // tm
    units = batch * (tm // SUPER) * N_HEADS
    lane_row = lambda vals, off: jnp.zeros((1, LANES), F32).at[0, off:off + N_HEADS].set(vals)
    tok = lambda width, col: pl.BlockSpec((batch, tm, width), lambda t: (0, t, col))
    full = lambda shape: pl.BlockSpec(shape, lambda t: (0,) * len(shape))
    proj3 = proj.reshape(batch, seq, proj.shape[1])
    out = pl.pallas_call(
        _delta_body,
        grid=(nt,),
        in_specs=[
            tok(3 * WIDTH, DN_OFF // (3 * WIDTH)),
            tok(WIDTH, (DN_OFF + 3 * WIDTH) // WIDTH),
            tok(LANES, 0),
            full((CONV_WIDTH, 3 * WIDTH)),
            full((1, LANES)),
            full((1, LANES)),
            full((1, HEAD_DIM)),
        ],
        out_specs=tok(WIDTH, 0),
        out_shape=jax.ShapeDtypeStruct((batch, seq, WIDTH), BF16),
        scratch_shapes=[
            pltpu.VMEM((batch, 2 * SUBLANES, 3 * WIDTH), F32),
            pltpu.VMEM((batch, tm, 3 * WIDTH), F32),
            pltpu.VMEM((batch, tm, LANES), F32),
            pltpu.VMEM((batch, tm, LANES), F32),
            pltpu.VMEM((batch * N_HEADS, HEAD_DIM, HEAD_DIM), F32),
            pltpu.VMEM((batch, tm, WIDTH), F32),
            pltpu.VMEM((batch, tm, WIDTH), BF16),
            pltpu.VMEM((batch, tm, WIDTH), BF16),
            pltpu.VMEM((units, SUPER, SUPER), BF16),
            pltpu.VMEM((units, HEAD_DIM, SUPER), BF16),
            pltpu.VMEM((2 * units, SUBLANES, LANES), F32),
        ],
        compiler_params=pltpu.CompilerParams(
            dimension_semantics=("arbitrary",), vmem_limit_bytes=VMEM_LIMIT),
        name="delta",
    )(proj3, proj3, small.reshape(batch, seq, LANES), conv_w,
      lane_row(a_log, ALPHA_LANE), lane_row(dt_bias, ALPHA_LANE), norm_w.reshape(1, HEAD_DIM))
    return out.reshape(n, WIDTH)


def _sg_tile(u_ref, v_ref, z_ref, lnw_ref, lnb_ref, ws_ref, bs_ref):
    tm = u_ref.shape[0]
    v = v_ref[...].astype(F32)
    mu = jnp.mean(v, axis=-1, keepdims=True)
    vc = v - mu
    var = jnp.mean(vc * vc, axis=-1, keepdims=True)
    vn = (vc * lax.rsqrt(var + LN_EPS) * lnw_ref[...] + lnb_ref[...]).astype(BF16)
    row = lax.broadcasted_iota(jnp.int32, (SG_CHUNK, SG_CHUNK), 0)
    col = lax.broadcasted_iota(jnp.int32, (SG_CHUNK, SG_CHUNK), 1)
    tril = col <= row
    gate = u_ref[...].astype(F32) * _silu(z_ref[...].astype(F32))
    bs = bs_ref[...]
    cols = []
    for g in range(N_HEADS):
        wg = jnp.where(tril, ws_ref[g], 0.0).astype(BF16)
        bg = bs[:, g:g + 1]
        cs = slice(g * HEAD_DIM, (g + 1) * HEAD_DIM)
        chunks = []
        for c in range(tm // SG_CHUNK):
            rs = slice(c * SG_CHUNK, (c + 1) * SG_CHUNK)
            mixed = jnp.dot(wg, vn[rs, cs], preferred_element_type=F32) + bg
            chunks.append((gate[rs, cs] * mixed).astype(BF16))
        cols.append(jnp.concatenate(chunks, axis=0))
    return jnp.concatenate(cols, axis=1)


LOG2E = 1.4426950408889634
ATTN_SUB = 256
ATTN_STAGES_PER_REGION = 6


def _fox_prep_body(q_ref, k_ref, v_ref, small_ref, fb_ref, qw_ref, kw_ref,
                   qn_ref, kn_ref, vt_ref, ccol_ref, crow_ref, carry_ref):
    t = pl.program_id(1)
    tm = q_ref.shape[0]

    @pl.when(t == 0)
    def _():
        carry_ref[...] = jnp.zeros(carry_ref.shape, F32)

    for h in range(N_HEADS):
        cs = slice(h * HEAD_DIM, (h + 1) * HEAD_DIM)
        q = q_ref[:, cs].astype(F32)
        k = k_ref[:, cs].astype(F32)
        q = q * lax.rsqrt(jnp.mean(q * q, axis=-1, keepdims=True) + NORM_EPS) * qw_ref[...]
        k = k * lax.rsqrt(jnp.mean(k * k, axis=-1, keepdims=True) + NORM_EPS) * kw_ref[...]
        qn_ref[:, cs] = (q * (HEAD_DIM ** -0.5 * LOG2E)).astype(BF16)
        kn_ref[:, cs] = k.astype(BF16)
        vt_ref[0, h, 0] = v_ref[:, cs].astype(F32).T.astype(BF16)

    logf = _log_sigmoid(small_ref[...] + fb_ref[...])
    row = lax.broadcasted_iota(jnp.int32, (tm, tm), 0)
    col = lax.broadcasted_iota(jnp.int32, (tm, tm), 1)
    c = _dot_exact_lhs(jnp.where(col <= row, 1.0, 0.0), logf) + carry_ref[0:1, :]
    carry_ref[0:1, :] = c[tm - 1:tm, :]
    c2 = c * LOG2E
    ct = c2.T
    for h in range(N_HEADS):
        lane = FORGET_LANE + h
        ccol_ref[:, h * HEAD_DIM:(h + 1) * HEAD_DIM] = jnp.broadcast_to(c2[:, lane:lane + 1], (tm, HEAD_DIM))
        crow_ref[h * SUBLANES:(h + 1) * SUBLANES, :] = jnp.broadcast_to(ct[lane:lane + 1, :], (SUBLANES, tm))


def _fox_prep(proj, small, f_bias, qn_w, kn_w, batch, seq, tm):
    n = proj.shape[0]
    nt = seq // tm
    tok = lambda width, col: pl.BlockSpec((tm, width), lambda b, t: (b * nt + t, col))
    full = lambda shape: pl.BlockSpec(shape, lambda b, t: (0,) * len(shape))
    fb = jnp.zeros((1, LANES), F32).at[0, FORGET_LANE:FORGET_LANE + N_HEADS].set(f_bias)
    out_tok = pl.BlockSpec((tm, WIDTH), lambda b, t: (b * nt + t, 0))
    return pl.pallas_call(
        _fox_prep_body,
        grid=(batch, nt),
        in_specs=[
            tok(WIDTH, FOX_OFF // WIDTH), tok(WIDTH, FOX_OFF // WIDTH + 1), tok(WIDTH, FOX_OFF // WIDTH + 2),
            tok(LANES, 0),
            full((1, LANES)), full((1, HEAD_DIM)), full((1, HEAD_DIM)),
        ],
        out_specs=[out_tok, out_tok,
                   pl.BlockSpec((1, N_HEADS, 1, HEAD_DIM, tm), lambda b, t: (b, 0, t, 0, 0)),
                   out_tok,
                   pl.BlockSpec((N_HEADS * SUBLANES, tm), lambda b, t: (b, t))],
        out_shape=[jax.ShapeDtypeStruct((n, WIDTH), BF16)] * 2
        + [jax.ShapeDtypeStruct((batch, N_HEADS, nt, HEAD_DIM, tm), BF16),
           jax.ShapeDtypeStruct((n, WIDTH), F32),
           jax.ShapeDtypeStruct((batch * N_HEADS * SUBLANES, seq), F32)],
        scratch_shapes=[pltpu.VMEM((SUBLANES, LANES), F32)],
        compiler_params=pltpu.CompilerParams(
            dimension_semantics=("parallel", "arbitrary"), vmem_limit_bytes=VMEM_LIMIT),
        name="fox_prep",
    )(proj, proj, proj, small, fb, qn_w.reshape(1, HEAD_DIM), kn_w.reshape(1, HEAD_DIM))


def _fox_attn_body(go_ref, q_ref, k_ref, vt_ref, cq_ref, ck_ref, z_ref, o_ref,
                   m_ref, l_ref, acc_ref, sta_ref, stb_ref):
    blk = sta_ref.shape[1]
    nblk = q_ref.shape[0] // blk
    subs = [slice(s * ATTN_SUB, (s + 1) * ATTN_SUB) for s in range(blk // ATTN_SUB)]
    pairs = [(i, j) for i in range(nblk) for j in range(i + 1)]
    bufs = (sta_ref, stb_ref)

    def key_counts(i, j):
        return [s.stop if j == i else blk for s in subs]

    def scores(i, j, st_ref, si):
        s, nk = subs[si], key_counts(i, j)[si]
        rows = slice(j * blk, j * blk + nk)
        ck = jnp.tile(ck_ref[rows, :], (1, ATTN_SUB // HEAD_DIM))
        st_ref[si, 0:nk, :] = lax.dot_general(
            k_ref[rows, :], q_ref[i * blk + s.start:i * blk + s.stop, :], (((1,), (1,)), ((), ())),
            preferred_element_type=F32) - ck

    def softmax_pv(i, j, st_ref, si):
        s, nk = subs[si], key_counts(i, j)[si]
        st = st_ref[si, 0:nk, :]
        cq = cq_ref[0:1, i * blk + s.start:i * blk + s.stop]
        if j == i:
            kpos = lax.broadcasted_iota(jnp.int32, (nk, ATTN_SUB), 0)
            qpos = lax.broadcasted_iota(jnp.int32, (nk, ATTN_SUB), 1)
            st = jnp.where(kpos <= qpos + s.start, st, -jnp.inf)
        m_new = jnp.max(st, axis=0, keepdims=True) + cq
        if j > 0:
            m_prev = m_ref[0:1, s]
            m_new = jnp.maximum(m_prev, m_new)
            alpha = jnp.exp2(m_prev - m_new)
        p = jnp.exp2(st + (cq - m_new))
        lhs = jnp.concatenate([vt_ref[0, 0, j, :, 0:nk], jnp.ones((2 * SUBLANES, nk), BF16)], axis=0)
        pvs = jnp.dot(lhs, p.astype(BF16), preferred_element_type=F32)
        pv, psum = pvs[0:HEAD_DIM, :], pvs[HEAD_DIM:HEAD_DIM + 1, :]
        if j == 0:
            l_ref[0:1, s] = psum
            acc_ref[:, s] = pv
        else:
            l_ref[0:1, s] = alpha * l_ref[0:1, s] + psum
            acc_ref[:, s] = alpha * acc_ref[:, s] + pv
        m_ref[0:1, s] = m_new

    def finish(i):
        rows = slice(i * blk, (i + 1) * blk)
        o = (acc_ref[...] / l_ref[0:1, :]).T
        o_ref[rows, :] = (o * _silu(z_ref[rows, :].astype(F32))).astype(o_ref.dtype)

    def stage(n):
        for si in range(len(subs)):
            if n + 1 < len(pairs):
                scores(*pairs[n + 1], bufs[(n + 1) % 2], si)
            if n >= 0:
                softmax_pv(*pairs[n], bufs[n % 2], si)
        if n >= 0 and pairs[n][0] == pairs[n][1]:
            finish(pairs[n][0])

    def region(stages):
        @pl.when(go_ref[0] > 0)
        def _():
            for n in stages:
                stage(n)

    order = list(range(-1, len(pairs)))
    for r in range(0, len(order), ATTN_STAGES_PER_REGION):
        region(order[r:r + ATTN_STAGES_PER_REGION])


def _fox_attention(qn, kn, vt, ccol, crow, proj, batch, seq, blk):
    n = qn.shape[0]
    nb = seq // blk
    assert vt.shape == (batch, N_HEADS, nb, HEAD_DIM, blk)
    seq_spec = lambda arr_col: pl.BlockSpec((seq, HEAD_DIM), lambda b, h: (b, arr_col(h)))
    return pl.pallas_call(
        _fox_attn_body,
        grid=(batch, N_HEADS),
        in_specs=[
            pl.BlockSpec(memory_space=pltpu.SMEM),
            seq_spec(lambda h: h), seq_spec(lambda h: h),
            pl.BlockSpec((1, 1, nb, HEAD_DIM, blk), lambda b, h: (b, h, 0, 0, 0)),
            pl.BlockSpec((SUBLANES, seq), lambda b, h: (b * N_HEADS + h, 0)),
            seq_spec(lambda h: h),
            seq_spec(lambda h: (FOX_OFF + 3 * WIDTH) // HEAD_DIM + h),
        ],
        out_specs=seq_spec(lambda h: h),
        out_shape=jax.ShapeDtypeStruct((n, WIDTH), BF16),
        scratch_shapes=[pltpu.VMEM((SUBLANES, blk), F32), pltpu.VMEM((SUBLANES, blk), F32),
                        pltpu.VMEM((HEAD_DIM, blk), F32),
                        pltpu.VMEM((blk // ATTN_SUB, blk, ATTN_SUB), F32),
                        pltpu.VMEM((blk // ATTN_SUB, blk, ATTN_SUB), F32)],
        compiler_params=pltpu.CompilerParams(
            dimension_semantics=("parallel", "parallel"), vmem_limit_bytes=VMEM_LIMIT),
        name="fox_attention",
    )(jnp.ones((1,), jnp.int32), qn, kn, vt, crow, ccol, proj)


def _merge_body(od_ref, of_ref, g_ref, x_ref, wb_ref, wo_ref, nw_ref,
                u_ref, v_ref, z_ref, lnw_ref, lnb_ref, ws_ref, bs_ref, *out_refs):
    branches = (od_ref[...], _sg_tile(u_ref, v_ref, z_ref, lnw_ref, lnb_ref, ws_ref, bs_ref), of_ref[...])
    merged = None
    for nbr, branch in enumerate(branches):
        up = jnp.dot(branch, wb_ref[nbr], preferred_element_type=F32)
        term = _sigmoid(g_ref[:, nbr * D_MODEL:(nbr + 1) * D_MODEL].astype(F32)) * up
        merged = term if merged is None else merged + term
    x_new = x_ref[...] + jnp.dot(merged.astype(BF16), wo_ref[...], preferred_element_type=F32)
    normed_ref = out_refs[-1]
    normed_ref[...] = _rms_norm(x_new, nw_ref[...]).astype(normed_ref.dtype)
    if len(out_refs) == 2:
        out_refs[0][...] = x_new


def _merge(o_delta, o_fox, proj, x2d, w_branch, w_out, next_norm_w, sg_ln_w, sg_ln_b, w_s, b_s,
           layer, last, tm):
    n = x2d.shape[0]
    assert tm % SG_CHUNK == 0
    tok = lambda width, col: pl.BlockSpec((tm, width), lambda i: (i, col))
    full = lambda shape: pl.BlockSpec(shape, lambda i: (0,) * len(shape))
    bs_t = jnp.zeros((SG_CHUNK, LANES), F32).at[:, :N_HEADS].set(b_s.T)
    if last:
        out_specs = [tok(D_MODEL, 0)]
        out_shape = [jax.ShapeDtypeStruct((n, D_MODEL), F32)]
    else:
        out_specs = [tok(D_MODEL, 0), tok(D_MODEL, 0)]
        out_shape = [jax.ShapeDtypeStruct((n, D_MODEL), F32), jax.ShapeDtypeStruct((n, D_MODEL), BF16)]
    return pl.pallas_call(
        _merge_body,
        grid=(n // tm,),
        in_specs=[
            tok(WIDTH, 0), tok(WIDTH, 0),
            tok(N_BRANCHES * D_MODEL, GATES_OFF // (N_BRANCHES * D_MODEL)),
            tok(D_MODEL, 0),
            pl.BlockSpec((None, N_BRANCHES, WIDTH, D_MODEL), lambda i: (layer, 0, 0, 0)),
            pl.BlockSpec((None, D_MODEL, D_MODEL), lambda i: (layer, 0, 0)),
            full((1, D_MODEL)),
            tok(WIDTH, SG_OFF // WIDTH), tok(WIDTH, SG_OFF // WIDTH + 1), tok(WIDTH, SG_OFF // WIDTH + 2),
            full((1, WIDTH)), full((1, WIDTH)),
            full((N_HEADS, SG_CHUNK, SG_CHUNK)), full((SG_CHUNK, LANES)),
        ],
        out_specs=out_specs,
        out_shape=out_shape,
        compiler_params=pltpu.CompilerParams(
            dimension_semantics=("parallel",), vmem_limit_bytes=VMEM_LIMIT),
        name="merge",
    )(o_delta, o_fox, proj, x2d, w_branch, w_out, next_norm_w.reshape(1, D_MODEL),
      proj, proj, proj, sg_ln_w.reshape(1, WIDTH), sg_ln_b.reshape(1, WIDTH), w_s, bs_t)


def _first_norm_body(x_ref, w_ref, o_ref):
    o_ref[...] = _rms_norm(x_ref[...], w_ref[...]).astype(o_ref.dtype)


def _first_norm(x2d, w, tm):
    n = x2d.shape[0]
    return pl.pallas_call(
        _first_norm_body,
        grid=(n // tm,),
        in_specs=[pl.BlockSpec((tm, D_MODEL), lambda i: (i, 0)),
                  pl.BlockSpec((1, D_MODEL), lambda i: (0, 0))],
        out_specs=pl.BlockSpec((tm, D_MODEL), lambda i: (i, 0)),
        out_shape=jax.ShapeDtypeStruct((n, D_MODEL), BF16),
        compiler_params=pltpu.CompilerParams(dimension_semantics=("parallel",)),
        name="first_norm",
    )(x2d, w.reshape(1, D_MODEL))


_IN_SIZES = (3 * WIDTH, WIDTH, N_HEADS, N_HEADS, 2 * WIDTH, WIDTH, 3 * WIDTH, WIDTH, N_HEADS,
             N_BRANCHES * D_MODEL)
_IN_STARTS = tuple(sum(_IN_SIZES[:i]) for i in range(len(_IN_SIZES) + 1))
D_IN = _IN_STARTS[-1]
_PACK_RUNS = (
    (_IN_STARTS[9], _IN_SIZES[9], GATES_OFF),
    (_IN_STARTS[0], _IN_SIZES[0] + _IN_SIZES[1], DN_OFF),
    (_IN_STARTS[4], _IN_SIZES[4] + _IN_SIZES[5], SG_OFF),
    (_IN_STARTS[6], _IN_SIZES[6] + _IN_SIZES[7], FOX_OFF),
)
_SMALL_RUNS = (
    (_IN_STARTS[2], _IN_SIZES[2] + _IN_SIZES[3], BETA_LANE),
    (_IN_STARTS[8], _IN_SIZES[8], FORGET_LANE),
)


def _pack_body(w_ref, o_ref, small_ref):
    rows = w_ref.shape[1]
    for src, length, dst in _PACK_RUNS:
        o_ref[0, :, dst:dst + length] = w_ref[0, :, src:src + length].astype(BF16)
    small_ref[0] = jnp.zeros((rows, LANES), BF16)
    for src, length, dst in _SMALL_RUNS:
        small_ref[0, :, dst:dst + length] = w_ref[0, :, src:src + length].astype(BF16)


def _pack_w_in(w_in, rows):
    depth, d_model, d_in = w_in.shape
    assert d_in == D_IN and d_model % rows == 0
    return pl.pallas_call(
        _pack_body,
        grid=(depth, d_model // rows),
        in_specs=[pl.BlockSpec((1, rows, D_IN), lambda l, i: (l, i, 0))],
        out_specs=[pl.BlockSpec((1, rows, PACKED_COLS), lambda l, i: (l, i, 0)),
                   pl.BlockSpec((1, rows, LANES), lambda l, i: (l, i, 0))],
        out_shape=[jax.ShapeDtypeStruct((depth, d_model, PACKED_COLS), BF16),
                   jax.ShapeDtypeStruct((depth, d_model, LANES), BF16)],
        compiler_params=pltpu.CompilerParams(
            dimension_semantics=("parallel", "parallel"), vmem_limit_bytes=VMEM_LIMIT),
        name="pack_w_in",
    )(w_in)


def _tile(total, want):
    t = min(total, want)
    assert total % t == 0, (total, t)
    return t


def kernel(x, norm_w, w_in, f_bias, conv_w, a_log, dt_bias, dn_norm_w, sg_ln_w, sg_ln_b,
           w_spatial, b_spatial, fox_qnorm_w, fox_knorm_w, w_branch, w_out, final_norm_w):
    batch, seq, d_model = x.shape
    assert d_model == D_MODEL and seq % SUPER == 0
    n = batch * seq
    depth = w_in.shape[0]
    x2d = x.reshape(n, D_MODEL)
    tm_proj = _tile(n, 2048)
    tm_delta = _tile(seq, SUPER)
    tm_prep = _tile(seq, 1024)
    blk_attn = _tile(seq, 1024)
    tm_merge = _tile(n, 512)
    w_packed, w_small = _pack_w_in(w_in, _tile(D_MODEL, 128))
    w_branch = w_branch.astype(BF16)
    w_out = w_out.astype(BF16)
    h = _first_norm(x2d, norm_w[0], _tile(n, 1024))
    for l in range(depth):
        last = l == depth - 1
        proj, small = _inproj(h, w_packed, w_small, l, tm_proj)
        o_delta = _delta_branch(proj, small, conv_w[l], a_log[l], dt_bias[l], dn_norm_w[l],
                                batch, seq, tm_delta)
        qn, kn, vt, ccol, crow = _fox_prep(proj, small, f_bias[l], fox_qnorm_w[l], fox_knorm_w[l],
                                           batch, seq, tm_prep)
        o_fox = _fox_attention(qn, kn, vt, ccol, crow, proj, batch, seq, blk_attn)
        outs = _merge(o_delta, o_fox, proj, x2d, w_branch, w_out, final_norm_w if last else norm_w[l + 1],
                      sg_ln_w[l], sg_ln_b[l], w_spatial[l], b_spatial[l], l, last, tm_merge)
        if last:
            return outs[0].reshape(batch, seq, D_MODEL)
        x2d, h = outs
```

```python
import jax
import jax.numpy as jnp
from jax import lax
from jax.experimental import pallas as pl
from jax.experimental.pallas import tpu as pltpu

F32 = jnp.float32
BF16 = jnp.bfloat16

D_MODEL = 1024
HEAD_DIM = 128
N_HEADS = 4
WIDTH = N_HEADS * HEAD_DIM
N_BRANCHES = 3
DN_CHUNK = 64
SUPER = 2 * DN_CHUNK
CONV_WIDTH = 4
SG_CHUNK = 128
NORM_EPS = 1e-6
LN_EPS = 1e-5
L2_EPS = 1e-6
LANES = 128
SUBLANES = 8

GATES_OFF = 0
DN_OFF = GATES_OFF + N_BRANCHES * D_MODEL
SG_OFF = DN_OFF + 4 * WIDTH
FOX_OFF = SG_OFF + 3 * WIDTH
PACKED_COLS = FOX_OFF + 4 * WIDTH
PROJ_TILE_N = PACKED_COLS // 4
assert PROJ_TILE_N % LANES == 0
BETA_LANE, ALPHA_LANE, FORGET_LANE = 0, N_HEADS, 2 * N_HEADS

VMEM_LIMIT = 48 * 1024 * 1024


def _dot(a, b):
    return jnp.dot(a.astype(BF16), b.astype(BF16), preferred_element_type=F32)


def _dot_nt(a, b):
    return lax.dot_general(a.astype(BF16), b.astype(BF16), (((1,), (1,)), ((), ())),
                           preferred_element_type=F32)


def _dot_exact_lhs(a01, x):
    a = a01.astype(BF16)
    x1 = x.astype(BF16)
    r1 = x - x1.astype(F32)
    x2 = r1.astype(BF16)
    x3 = (r1 - x2.astype(F32)).astype(BF16)
    d = lambda p: jnp.dot(a, p, preferred_element_type=F32)
    return d(x1) + d(x2) + d(x3)


def _sigmoid(x):
    return 1.0 / (1.0 + jnp.exp(-x))


def _silu(x):
    return x * _sigmoid(x)


def _softplus(x):
    return jnp.maximum(x, 0.0) + jnp.log(1.0 + jnp.exp(-jnp.abs(x)))


def _log_sigmoid(x):
    return -_softplus(-x)


def _rms_norm(x, w):
    return x * lax.rsqrt(jnp.mean(x * x, axis=-1, keepdims=True) + NORM_EPS) * w


def _inproj_body(h_ref, w_ref, ws_ref, o_ref, small_ref):
    @pl.when(pl.program_id(1) == 0)
    def _():
        small_ref[...] = jnp.dot(h_ref[...], ws_ref[...], preferred_element_type=F32)

    o_ref[...] = jnp.dot(h_ref[...], w_ref[...], preferred_element_type=F32).astype(o_ref.dtype)


def _inproj(h, w_packed, w_small, layer, tm):
    n = h.shape[0]
    return pl.pallas_call(
        _inproj_body,
        grid=(n // tm, PACKED_COLS // PROJ_TILE_N),
        in_specs=[
            pl.BlockSpec((tm, D_MODEL), lambda i, j: (i, 0)),
            pl.BlockSpec((None, D_MODEL, PROJ_TILE_N), lambda i, j: (layer, 0, j)),
            pl.BlockSpec((None, D_MODEL, LANES), lambda i, j: (layer, 0, 0)),
        ],
        out_specs=[pl.BlockSpec((tm, PROJ_TILE_N), lambda i, j: (i, j)),
                   pl.BlockSpec((tm, LANES), lambda i, j: (i, 0))],
        out_shape=[jax.ShapeDtypeStruct((n, PACKED_COLS), BF16),
                   jax.ShapeDtypeStruct((n, LANES), F32)],
        compiler_params=pltpu.CompilerParams(
            dimension_semantics=("parallel", "arbitrary"), vmem_limit_bytes=VMEM_LIMIT),
        name="inproj",
    )(h, w_packed, w_small)


def _unit_lower_inverse(lows, eye, m8, moffs):
    each = lambda f, *lists: [f(*args) for args in zip(*lists)]
    l8 = each(lambda low: jnp.where(m8, low, 0.0), lows)
    l8_2 = each(_dot, l8, l8)
    l8_3 = each(_dot, l8, l8_2)
    l8_4 = each(_dot, l8_2, l8_2)
    p1 = each(lambda a, b, c: eye - a + b - c, l8, l8_2, l8_3)
    x = each(lambda p, l4: p + _dot(p, l4), p1, l8_4)
    for moff in moffs:
        xl = each(lambda xi, low: _dot(xi, jnp.where(moff, low, 0.0)), x, lows)
        x = each(lambda xi, xli: xi - _dot(xli, xi), x, xl)
    return x


def _delta_body(qkv_ref, z_ref, small_ref, convw_ref, alog_ref, dtb_ref, nw_ref, o_ref,
                halo_ref, y_ref, bt_ref, gt_ref, s_ref, u_ref, w_ref, qd_ref, qk_ref, kdt_ref, gl_ref):
    t = pl.program_id(0)
    nb, tm = qkv_ref.shape[0], qkv_ref.shape[1]
    assert tm & (tm - 1) == 0
    nsc = tm // SUPER
    halo = SUBLANES
    taps = CONV_WIDTH - 1

    @pl.when(t == 0)
    def _():
        halo_ref[...] = jnp.zeros(halo_ref.shape, F32)
        s_ref[...] = jnp.zeros(s_ref.shape, F32)

    srow = lax.broadcasted_iota(jnp.int32, (taps * tm, tm), 0)
    scol = lax.broadcasted_iota(jnp.int32, (taps * tm, tm), 1)
    tap = jnp.right_shift(srow, tm.bit_length() - 1)
    shifts = jnp.where((srow - tap * tm) - scol == tap + 1, 1.0, 0.0).astype(BF16)
    for b in range(nb):
        xb = qkv_ref[b]
        xf = xb.astype(F32)
        shifted = jnp.dot(shifts, xb, preferred_element_type=F32)
        acc = convw_ref[taps:taps + 1, :] * xf
        head = None
        for s in range(1, taps + 1):
            wrow = convw_ref[taps - s:taps - s + 1, :]
            acc = acc + wrow * shifted[(s - 1) * tm:s * tm, :]
            patch = wrow * halo_ref[b, halo - s:2 * halo - s, :]
            head = patch if head is None else head + patch
        y_ref[b] = _silu(acc)
        y_ref[b, 0:halo, :] = _silu(acc[0:halo, :] + head)
        halo_ref[b, 0:halo, :] = xf[tm - halo:tm, :]
        sm = small_ref[b]
        bt_ref[b] = _sigmoid(sm)
        gt_ref[b] = -jnp.exp(alog_ref[...]) * _softplus(sm + dtb_ref[...])

    row = lax.broadcasted_iota(jnp.int32, (SUPER, SUPER), 0)
    col = lax.broadcasted_iota(jnp.int32, (SUPER, SUPER), 1)
    blk = lambda idx, size: jnp.right_shift(idx, size.bit_length() - 1)
    same_chunk = blk(row, DN_CHUNK) == blk(col, DN_CHUNK)
    causal = same_chunk & (col <= row)
    strict = same_chunk & (col < row)
    causal_ones = jnp.where(causal, 1.0, 0.0).astype(BF16)
    eye = jnp.where(row == col, 1.0, 0.0).astype(F32)
    m8 = blk(row, 8) == blk(col, 8)
    moffs = [(blk(row, 2 * s) == blk(col, 2 * s)) & (blk(row, s) != blk(col, s)) for s in (8, 16, 32)]
    first_rows = row < DN_CHUNK
    first_cols = col < DN_CHUNK
    nw = nw_ref[...]

    each = lambda f, *lists: [f(*args) for args in zip(*lists)]
    hcols = lambda h: slice(h * HEAD_DIM, (h + 1) * HEAD_DIM)
    srows = lambda sc: slice(sc * SUPER, (sc + 1) * SUPER)
    unit = lambda b, sc, h: (b * nsc + sc) * N_HEADS + h
    units = [(b, sc, h) for b in range(nb) for sc in range(nsc) for h in range(N_HEADS)]
    kbs, ks, qs, decs, egcs = [], [], [], [], []
    for b in range(nb):
        for sc in range(nsc):
            rows = srows(sc)
            gc = _dot_exact_lhs(causal_ones, gt_ref[b, rows, :])
            gct = gc.T
            bt = bt_ref[b, rows, :]
            egc = jnp.exp(gc)
            gl0 = gc[DN_CHUNK - 1:DN_CHUNK, :]
            gl1 = gc[SUPER - 1:SUPER, :]
            ekd = jnp.exp(jnp.where(first_rows, gl0, gl1) - gc)
            egl = (jnp.exp(gl0), jnp.exp(gl1))
            for h in range(N_HEADS):
                lane = slice(ALPHA_LANE + h, ALPHA_LANE + h + 1)
                q = y_ref[b, rows, h * HEAD_DIM:(h + 1) * HEAD_DIM]
                k = y_ref[b, rows, WIDTH + h * HEAD_DIM:WIDTH + (h + 1) * HEAD_DIM]
                q = q * lax.rsqrt(jnp.sum(q * q, axis=-1, keepdims=True) + L2_EPS) * (HEAD_DIM ** -0.5)
                k = k * lax.rsqrt(jnp.sum(k * k, axis=-1, keepdims=True) + L2_EPS)
                dec = jnp.exp(jnp.where(causal, gc[:, lane] - gct[lane, :], -jnp.inf))
                kb = k * bt[:, BETA_LANE + h:BETA_LANE + h + 1]
                qd_ref[b, rows, hcols(h)] = (q * egc[:, lane]).astype(BF16)
                kdt_ref[unit(b, sc, h)] = (k * ekd[:, lane]).T.astype(BF16)
                for c in range(2):
                    gl_ref[unit(b, sc, h) * 2 + c] = jnp.broadcast_to(egl[c][:, lane], (SUBLANES, LANES))
                qs.append(q), ks.append(k), kbs.append(kb), decs.append(dec), egcs.append(egc[:, lane])
    lows = each(lambda kb, k, dec: jnp.where(strict, _dot_nt(kb, k) * dec, 0.0), kbs, ks, decs)
    qks = each(lambda q, k, dec: (_dot_nt(q, k) * dec).astype(BF16), qs, ks, decs)
    tmats = _unit_lower_inverse(lows, eye, m8, moffs)
    for (b, sc, h), tmat, kb, egc_col, qk in zip(units, tmats, kbs, egcs, qks):
        rows = srows(sc)
        v = y_ref[b, rows, 2 * WIDTH + h * HEAD_DIM:2 * WIDTH + (h + 1) * HEAD_DIM]
        u_ref[b, rows, hcols(h)] = _dot(tmat, v * bt_ref[b, rows, BETA_LANE + h:BETA_LANE + h + 1])
        w_ref[b, rows, hcols(h)] = _dot(tmat, kb * egc_col).astype(BF16)
        qk_ref[unit(b, sc, h)] = qk

    chains = [(b, h) for b in range(nb) for h in range(N_HEADS)]
    zero = jnp.zeros((HEAD_DIM, SUPER), BF16)
    for sc in range(nsc):
        rows = srows(sc)
        u = [u_ref[b, rows, hcols(h)] for b, h in chains]
        w = [w_ref[b, rows, hcols(h)] for b, h in chains]
        kdt = [kdt_ref[unit(b, sc, h)] for b, h in chains]
        kdt0 = [jnp.where(first_cols, x, zero) for x in kdt]
        kdt1 = [jnp.where(first_cols, zero, x) for x in kdt]
        g0 = [jnp.tile(gl_ref[unit(b, sc, h) * 2], (HEAD_DIM // SUBLANES, 1)) for b, h in chains]
        g1 = [jnp.tile(gl_ref[unit(b, sc, h) * 2 + 1], (HEAD_DIM // SUBLANES, 1)) for b, h in chains]
        s0 = [s_ref[b * N_HEADS + h] for b, h in chains]
        vn0 = each(lambda ui, wi, si: ui - _dot(wi, si), u, w, s0)
        s1 = each(lambda si, gi, ki, vi: si * gi + _dot(ki, vi), s0, g0, kdt0, vn0)
        vn1 = each(lambda ui, wi, si: ui - _dot(wi, si), u, w, s1)
        s2 = each(lambda si, gi, ki, vi: si * gi + _dot(ki, vi), s1, g1, kdt1, vn1)
        for ci, (b, h) in enumerate(chains):
            s_ref[b * N_HEADS + h] = s2[ci]
        qd = [qd_ref[b, rows, hcols(h)] for b, h in chains]
        o0 = each(_dot, qd, s0)
        o1 = each(_dot, qd, s1)
        vn = each(lambda a, c: jnp.where(first_rows, a, c), vn0, vn1)
        oq = each(lambda qki, vi: _dot(qki, vi), [qk_ref[unit(b, sc, h)] for b, h in chains], vn)
        for ci, (b, h) in enumerate(chains):
            o = jnp.where(first_rows, o0[ci], o1[ci]) + oq[ci]
            o = o * lax.rsqrt(jnp.mean(o * o, axis=-1, keepdims=True) + NORM_EPS) * nw
            o_ref[b, rows, hcols(h)] = (o * _silu(z_ref[b, rows, hcols(h)].astype(F32))).astype(o_ref.dtype)


def _delta_branch(proj, small, conv_w, a_log, dt_bias, norm_w, batch, seq, tm):
    n = proj.shape[0]
    nt = seq // tm
    units = batch * (tm // SUPER) * N_HEADS
    lane_row = lambda vals, off: jnp.zeros((1, LANES), F32).at[0, off:off + N_HEADS].set(vals)
    tok = lambda width, col: pl.BlockSpec((batch, tm, width), lambda t: (0, t, col))
    full = lambda shape: pl.BlockSpec(shape, lambda t: (0,) * len(shape))
    proj3 = proj.reshape(batch, seq, proj.shape[1])
    out = pl.pallas_call(
        _delta_body,
        grid=(nt,),
        in_specs=[
            tok(3 * WIDTH, DN_OFF // (3 * WIDTH)),
            tok(WIDTH, (DN_OFF + 3 * WIDTH) // WIDTH),
            tok(LANES, 0),
            full((CONV_WIDTH, 3 * WIDTH)),
            full((1, LANES)),
            full((1, LANES)),
            full((1, HEAD_DIM)),
        ],
        out_specs=tok(WIDTH, 0),
        out_shape=jax.ShapeDtypeStruct((batch, seq, WIDTH), BF16),
        scratch_shapes=[
            pltpu.VMEM((batch, 2 * SUBLANES, 3 * WIDTH), F32),
            pltpu.VMEM((batch, tm, 3 * WIDTH), F32),
            pltpu.VMEM((batch, tm, LANES), F32),
            pltpu.VMEM((batch, tm, LANES), F32),
            pltpu.VMEM((batch * N_HEADS, HEAD_DIM, HEAD_DIM), F32),
            pltpu.VMEM((batch, tm, WIDTH), F32),
            pltpu.VMEM((batch, tm, WIDTH), BF16),
            pltpu.VMEM((batch, tm, WIDTH), BF16),
            pltpu.VMEM((units, SUPER, SUPER), BF16),
            pltpu.VMEM((units, HEAD_DIM, SUPER), BF16),
            pltpu.VMEM((2 * units, SUBLANES, LANES), F32),
        ],
        compiler_params=pltpu.CompilerParams(
            dimension_semantics=("arbitrary",), vmem_limit_bytes=VMEM_LIMIT),
        name="delta",
    )(proj3, proj3, small.reshape(batch, seq, LANES), conv_w,
      lane_row(a_log, ALPHA_LANE), lane_row(dt_bias, ALPHA_LANE), norm_w.reshape(1, HEAD_DIM))
    return out.reshape(n, WIDTH)


def _sg_tile(u_ref, v_ref, z_ref, lnw_ref, lnb_ref, ws_ref, bs_ref):
    tm = u_ref.shape[0]
    v = v_ref[...].astype(F32)
    mu = jnp.mean(v, axis=-1, keepdims=True)
    vc = v - mu
    var = jnp.mean(vc * vc, axis=-1, keepdims=True)
    vn = (vc * lax.rsqrt(var + LN_EPS) * lnw_ref[...] + lnb_ref[...]).astype(BF16)
    row = lax.broadcasted_iota(jnp.int32, (SG_CHUNK, SG_CHUNK), 0)
    col = lax.broadcasted_iota(jnp.int32, (SG_CHUNK, SG_CHUNK), 1)
    tril = col <= row
    gate = u_ref[...].astype(F32) * _silu(z_ref[...].astype(F32))
    bs = bs_ref[...]
    cols = []
    for g in range(N_HEADS):
        wg = jnp.where(tril, ws_ref[g], 0.0).astype(BF16)
        bg = bs[:, g:g + 1]
        cs = slice(g * HEAD_DIM, (g + 1) * HEAD_DIM)
        chunks = []
        for c in range(tm // SG_CHUNK):
            rs = slice(c * SG_CHUNK, (c + 1) * SG_CHUNK)
            mixed = jnp.dot(wg, vn[rs, cs], preferred_element_type=F32) + bg
            chunks.append((gate[rs, cs] * mixed).astype(BF16))
        cols.append(jnp.concatenate(chunks, axis=0))
    return jnp.concatenate(cols, axis=1)


LOG2E = 1.4426950408889634
ATTN_SUB = 256
ATTN_STAGES_PER_REGION = 6


def _fox_prep_body(q_ref, k_ref, v_ref, small_ref, fb_ref, qw_ref, kw_ref,
                   qn_ref, kn_ref, vt_ref, ccol_ref, crow_ref, carry_ref):
    t = pl.program_id(1)
    tm = q_ref.shape[0]

    @pl.when(t == 0)
    def _():
        carry_ref[...] = jnp.zeros(carry_ref.shape, F32)

    for h in range(N_HEADS):
        cs = slice(h * HEAD_DIM, (h + 1) * HEAD_DIM)
        q = q_ref[:, cs].astype(F32)
        k = k_ref[:, cs].astype(F32)
        q = q * lax.rsqrt(jnp.mean(q * q, axis=-1, keepdims=True) + NORM_EPS) * qw_ref[...]
        k = k * lax.rsqrt(jnp.mean(k * k, axis=-1, keepdims=True) + NORM_EPS) * kw_ref[...]
        qn_ref[:, cs] = (q * (HEAD_DIM ** -0.5 * LOG2E)).astype(BF16)
        kn_ref[:, cs] = k.astype(BF16)
        vt_ref[0, h, 0] = v_ref[:, cs].astype(F32).T.astype(BF16)

    logf = _log_sigmoid(small_ref[...] + fb_ref[...])
    row = lax.broadcasted_iota(jnp.int32, (tm, tm), 0)
    col = lax.broadcasted_iota(jnp.int32, (tm, tm), 1)
    c = _dot_exact_lhs(jnp.where(col <= row, 1.0, 0.0), logf) + carry_ref[0:1, :]
    carry_ref[0:1, :] = c[tm - 1:tm, :]
    c2 = c * LOG2E
    ct = c2.T
    for h in range(N_HEADS):
        lane = FORGET_LANE + h
        ccol_ref[:, h * HEAD_DIM:(h + 1) * HEAD_DIM] = jnp.broadcast_to(c2[:, lane:lane + 1], (tm, HEAD_DIM))
        crow_ref[h * SUBLANES:(h + 1) * SUBLANES, :] = jnp.broadcast_to(ct[lane:lane + 1, :], (SUBLANES, tm))


def _fox_prep(proj, small, f_bias, qn_w, kn_w, batch, seq, tm):
    n = proj.shape[0]
    nt = seq // tm
    tok = lambda width, col: pl.BlockSpec((tm, width), lambda b, t: (b * nt + t, col))
    full = lambda shape: pl.BlockSpec(shape, lambda b, t: (0,) * len(shape))
    fb = jnp.zeros((1, LANES), F32).at[0, FORGET_LANE:FORGET_LANE + N_HEADS].set(f_bias)
    out_tok = pl.BlockSpec((tm, WIDTH), lambda b, t: (b * nt + t, 0))
    return pl.pallas_call(
        _fox_prep_body,
        grid=(batch, nt),
        in_specs=[
            tok(WIDTH, FOX_OFF // WIDTH), tok(WIDTH, FOX_OFF // WIDTH + 1), tok(WIDTH, FOX_OFF // WIDTH + 2),
            tok(LANES, 0),
            full((1, LANES)), full((1, HEAD_DIM)), full((1, HEAD_DIM)),
        ],
        out_specs=[out_tok, out_tok,
                   pl.BlockSpec((1, N_HEADS, 1, HEAD_DIM, tm), lambda b, t: (b, 0, t, 0, 0)),
                   out_tok,
                   pl.BlockSpec((N_HEADS * SUBLANES, tm), lambda b, t: (b, t))],
        out_shape=[jax.ShapeDtypeStruct((n, WIDTH), BF16)] * 2
        + [jax.ShapeDtypeStruct((batch, N_HEADS, nt, HEAD_DIM, tm), BF16),
           jax.ShapeDtypeStruct((n, WIDTH), F32),
           jax.ShapeDtypeStruct((batch * N_HEADS * SUBLANES, seq), F32)],
        scratch_shapes=[pltpu.VMEM((SUBLANES, LANES), F32)],
        compiler_params=pltpu.CompilerParams(
            dimension_semantics=("parallel", "arbitrary"), vmem_limit_bytes=VMEM_LIMIT),
        name="fox_prep",
    )(proj, proj, proj, small, fb, qn_w.reshape(1, HEAD_DIM), kn_w.reshape(1, HEAD_DIM))


def _fox_attn_body(go_ref, q_ref, k_ref, vt_ref, cq_ref, ck_ref, z_ref, o_ref,
                   m_ref, l_ref, acc_ref, sta_ref, stb_ref):
    blk = sta_ref.shape[1]
    nblk = q_ref.shape[0] // blk
    subs = [slice(s * ATTN_SUB, (s + 1) * ATTN_SUB) for s in range(blk // ATTN_SUB)]
    pairs = [(i, j) for i in range(nblk) for j in range(i + 1)]
    bufs = (sta_ref, stb_ref)

    def key_counts(i, j):
        return [s.stop if j == i else blk for s in subs]

    def scores(i, j, st_ref, si):
        s, nk = subs[si], key_counts(i, j)[si]
        rows = slice(j * blk, j * blk + nk)
        ck = jnp.tile(ck_ref[rows, :], (1, ATTN_SUB // HEAD_DIM))
        st_ref[si, 0:nk, :] = lax.dot_general(
            k_ref[rows, :], q_ref[i * blk + s.start:i * blk + s.stop, :], (((1,), (1,)), ((), ())),
            preferred_element_type=F32) - ck

    def softmax_pv(i, j, st_ref, si):
        s, nk = subs[si], key_counts(i, j)[si]
        st = st_ref[si, 0:nk, :]
        cq = cq_ref[0:1, i * blk + s.start:i * blk + s.stop]
        if j == i:
            kpos = lax.broadcasted_iota(jnp.int32, (nk, ATTN_SUB), 0)
            qpos = lax.broadcasted_iota(jnp.int32, (nk, ATTN_SUB), 1)
            st = jnp.where(kpos <= qpos + s.start, st, -jnp.inf)
        m_new = jnp.max(st, axis=0, keepdims=True) + cq
        if j > 0:
            m_prev = m_ref[0:1, s]
            m_new = jnp.maximum(m_prev, m_new)
            alpha = jnp.exp2(m_prev - m_new)
        p = jnp.exp2(st + (cq - m_new))
        lhs = jnp.concatenate([vt_ref[0, 0, j, :, 0:nk], jnp.ones((2 * SUBLANES, nk), BF16)], axis=0)
        pvs = jnp.dot(lhs, p.astype(BF16), preferred_element_type=F32)
        pv, psum = pvs[0:HEAD_DIM, :], pvs[HEAD_DIM:HEAD_DIM + 1, :]
        if j == 0:
            l_ref[0:1, s] = psum
            acc_ref[:, s] = pv
        else:
            l_ref[0:1, s] = alpha * l_ref[0:1, s] + psum
            acc_ref[:, s] = alpha * acc_ref[:, s] + pv
        m_ref[0:1, s] = m_new

    def finish(i):
        rows = slice(i * blk, (i + 1) * blk)
        o = (acc_ref[...] / l_ref[0:1, :]).T
        o_ref[rows, :] = (o * _silu(z_ref[rows, :].astype(F32))).astype(o_ref.dtype)

    def stage(n):
        for si in range(len(subs)):
            if n + 1 < len(pairs):
                scores(*pairs[n + 1], bufs[(n + 1) % 2], si)
            if n >= 0:
                softmax_pv(*pairs[n], bufs[n % 2], si)
        if n >= 0 and pairs[n][0] == pairs[n][1]:
            finish(pairs[n][0])

    def region(stages):
        @pl.when(go_ref[0] > 0)
        def _():
            for n in stages:
                stage(n)

    order = list(range(-1, len(pairs)))
    for r in range(0, len(order), ATTN_STAGES_PER_REGION):
        region(order[r:r + ATTN_STAGES_PER_REGION])


def _fox_attention(qn, kn, vt, ccol, crow, proj, batch, seq, blk):
    n = qn.shape[0]
    nb = seq // blk
    assert vt.shape == (batch, N_HEADS, nb, HEAD_DIM, blk)
    seq_spec = lambda arr_col: pl.BlockSpec((seq, HEAD_DIM), lambda b, h: (b, arr_col(h)))
    return pl.pallas_call(
        _fox_attn_body,
        grid=(batch, N_HEADS),
        in_specs=[
            pl.BlockSpec(memory_space=pltpu.SMEM),
            seq_spec(lambda h: h), seq_spec(lambda h: h),
            pl.BlockSpec((1, 1, nb, HEAD_DIM, blk), lambda b, h: (b, h, 0, 0, 0)),
            pl.BlockSpec((SUBLANES, seq), lambda b, h: (b * N_HEADS + h, 0)),
            seq_spec(lambda h: h),
            seq_spec(lambda h: (FOX_OFF + 3 * WIDTH) // HEAD_DIM + h),
        ],
        out_specs=seq_spec(lambda h: h),
        out_shape=jax.ShapeDtypeStruct((n, WIDTH), BF16),
        scratch_shapes=[pltpu.VMEM((SUBLANES, blk), F32), pltpu.VMEM((SUBLANES, blk), F32),
                        pltpu.VMEM((HEAD_DIM, blk), F32),
                        pltpu.VMEM((blk // ATTN_SUB, blk, ATTN_SUB), F32),
                        pltpu.VMEM((blk // ATTN_SUB, blk, ATTN_SUB), F32)],
        compiler_params=pltpu.CompilerParams(
            dimension_semantics=("parallel", "parallel"), vmem_limit_bytes=VMEM_LIMIT),
        name="fox_attention",
    )(jnp.ones((1,), jnp.int32), qn, kn, vt, crow, ccol, proj)


def _merge_body(od_ref, of_ref, g_ref, x_ref, wb_ref, wo_ref, nw_ref,
                u_ref, v_ref, z_ref, lnw_ref, lnb_ref, ws_ref, bs_ref, *out_refs):
    branches = (od_ref[...], _sg_tile(u_ref, v_ref, z_ref, lnw_ref, lnb_ref, ws_ref, bs_ref), of_ref[...])
    merged = None
    for nbr, branch in enumerate(branches):
        up = jnp.dot(branch, wb_ref[nbr], preferred_element_type=F32)
        term = _sigmoid(g_ref[:, nbr * D_MODEL:(nbr + 1) * D_MODEL].astype(F32)) * up
        merged = term if merged is None else merged + term
    x_new = x_ref[...] + jnp.dot(merged.astype(BF16), wo_ref[...], preferred_element_type=F32)
    normed_ref = out_refs[-1]
    normed_ref[...] = _rms_norm(x_new, nw_ref[...]).astype(normed_ref.dtype)
    if len(out_refs) == 2:
        out_refs[0][...] = x_new


def _merge(o_delta, o_fox, proj, x2d, w_branch, w_out, next_norm_w, sg_ln_w, sg_ln_b, w_s, b_s,
           layer, last, tm):
    n = x2d.shape[0]
    assert tm % SG_CHUNK == 0
    tok = lambda width, col: pl.BlockSpec((tm, width), lambda i: (i, col))
    full = lambda shape: pl.BlockSpec(shape, lambda i: (0,) * len(shape))
    bs_t = jnp.zeros((SG_CHUNK, LANES), F32).at[:, :N_HEADS].set(b_s.T)
    if last:
        out_specs = [tok(D_MODEL, 0)]
        out_shape = [jax.ShapeDtypeStruct((n, D_MODEL), F32)]
    else:
        out_specs = [tok(D_MODEL, 0), tok(D_MODEL, 0)]
        out_shape = [jax.ShapeDtypeStruct((n, D_MODEL), F32), jax.ShapeDtypeStruct((n, D_MODEL), BF16)]
    return pl.pallas_call(
        _merge_body,
        grid=(n // tm,),
        in_specs=[
            tok(WIDTH, 0), tok(WIDTH, 0),
            tok(N_BRANCHES * D_MODEL, GATES_OFF // (N_BRANCHES * D_MODEL)),
            tok(D_MODEL, 0),
            pl.BlockSpec((None, N_BRANCHES, WIDTH, D_MODEL), lambda i: (layer, 0, 0, 0)),
            pl.BlockSpec((None, D_MODEL, D_MODEL), lambda i: (layer, 0, 0)),
            full((1, D_MODEL)),
            tok(WIDTH, SG_OFF // WIDTH), tok(WIDTH, SG_OFF // WIDTH + 1), tok(WIDTH, SG_OFF // WIDTH + 2),
            full((1, WIDTH)), full((1, WIDTH)),
            full((N_HEADS, SG_CHUNK, SG_CHUNK)), full((SG_CHUNK, LANES)),
        ],
        out_specs=out_specs,
        out_shape=out_shape,
        compiler_params=pltpu.CompilerParams(
            dimension_semantics=("parallel",), vmem_limit_bytes=VMEM_LIMIT),
        name="merge",
    )(o_delta, o_fox, proj, x2d, w_branch, w_out, next_norm_w.reshape(1, D_MODEL),
      proj, proj, proj, sg_ln_w.reshape(1, WIDTH), sg_ln_b.reshape(1, WIDTH), w_s, bs_t)


def _first_norm_body(x_ref, w_ref, o_ref):
    o_ref[...] = _rms_norm(x_ref[...], w_ref[...]).astype(o_ref.dtype)


def _first_norm(x2d, w, tm):
    n = x2d.shape[0]
    return pl.pallas_call(
        _first_norm_body,
        grid=(n // tm,),
        in_specs=[pl.BlockSpec((tm, D_MODEL), lambda i: (i, 0)),
                  pl.BlockSpec((1, D_MODEL), lambda i: (0, 0))],
        out_specs=pl.BlockSpec((tm, D_MODEL), lambda i: (i, 0)),
        out_shape=jax.ShapeDtypeStruct((n, D_MODEL), BF16),
        compiler_params=pltpu.CompilerParams(dimension_semantics=("parallel",)),
        name="first_norm",
    )(x2d, w.reshape(1, D_MODEL))


_IN_SIZES = (3 * WIDTH, WIDTH, N_HEADS, N_HEADS, 2 * WIDTH, WIDTH, 3 * WIDTH, WIDTH, N_HEADS,
             N_BRANCHES * D_MODEL)
_IN_STARTS = tuple(sum(_IN_SIZES[:i]) for i in range(len(_IN_SIZES) + 1))
D_IN = _IN_STARTS[-1]
_PACK_RUNS = (
    (_IN_STARTS[9], _IN_SIZES[9], GATES_OFF),
    (_IN_STARTS[0], _IN_SIZES[0] + _IN_SIZES[1], DN_OFF),
    (_IN_STARTS[4], _IN_SIZES[4] + _IN_SIZES[5], SG_OFF),
    (_IN_STARTS[6], _IN_SIZES[6] + _IN_SIZES[7], FOX_OFF),
)
_SMALL_RUNS = (
    (_IN_STARTS[2], _IN_SIZES[2] + _IN_SIZES[3], BETA_LANE),
    (_IN_STARTS[8], _IN_SIZES[8], FORGET_LANE),
)


PACK_TILE = LANES


def _pack_body(w_ref, o_ref):
    for l in range(o_ref.shape[0]):
        o_ref[l] = w_ref[:, l, :].T.astype(BF16)


def _pack_src_start(j):
    col = j * PACK_TILE
    start = jnp.int32(0)
    for src, length, dst in _PACK_RUNS:
        start = jnp.where((col >= dst) & (col < dst + length), col - dst + src, start)
    return start


def _pack_w_in(w_in):
    depth, d_model, d_in = w_in.shape
    assert d_in == D_IN and all(length % PACK_TILE == 0 and dst % PACK_TILE == 0 for _, length, dst in _PACK_RUNS)
    packed = pl.pallas_call(
        _pack_body,
        grid=(PACKED_COLS // PACK_TILE,),
        in_specs=[pl.BlockSpec((pl.Element(PACK_TILE), pl.Element(depth), pl.Element(d_model)),
                               lambda j: (_pack_src_start(j), 0, 0))],
        out_specs=pl.BlockSpec((depth, d_model, PACK_TILE), lambda j: (0, 0, j)),
        out_shape=jax.ShapeDtypeStruct((depth, d_model, PACKED_COLS), BF16),
        compiler_params=pltpu.CompilerParams(
            dimension_semantics=("parallel",), vmem_limit_bytes=VMEM_LIMIT),
        name="pack_w_in",
    )(jnp.transpose(w_in, (2, 0, 1)))
    small = jnp.zeros((depth, d_model, LANES), BF16)
    for src, length, dst in _SMALL_RUNS:
        small = small.at[:, :, dst:dst + length].set(w_in[:, :, src:src + length].astype(BF16))
    return packed, small


def _tile(total, want):
    t = min(total, want)
    assert total % t == 0, (total, t)
    return t


def kernel(x, norm_w, w_in, f_bias, conv_w, a_log, dt_bias, dn_norm_w, sg_ln_w, sg_ln_b,
           w_spatial, b_spatial, fox_qnorm_w, fox_knorm_w, w_branch, w_out, final_norm_w):
    batch, seq, d_model = x.shape
    assert d_model == D_MODEL and seq % SUPER == 0
    n = batch * seq
    depth = w_in.shape[0]
    x2d = x.reshape(n, D_MODEL)
    tm_proj = _tile(n, 2048)
    tm_delta = _tile(seq, SUPER)
    tm_prep = _tile(seq, 1024)
    blk_attn = _tile(seq, 1024)
    tm_merge = _tile(n, 512)
    w_packed, w_small = _pack_w_in(w_in)
    w_branch = w_branch.astype(BF16)
    w_out = w_out.astype(BF16)
    h = _first_norm(x2d, norm_w[0], _tile(n, 1024))
    for l in range(depth):
        last = l == depth - 1
        proj, small = _inproj(h, w_packed, w_small, l, tm_proj)
        o_delta = _delta_branch(proj, small, conv_w[l], a_log[l], dt_bias[l], dn_norm_w[l],
                                batch, seq, tm_delta)
        qn, kn, vt, ccol, crow = _fox_prep(proj, small, f_bias[l], fox_qnorm_w[l], fox_knorm_w[l],
                                           batch, seq, tm_prep)
        o_fox = _fox_attention(qn, kn, vt, ccol, crow, proj, batch, seq, blk_attn)
        outs = _merge(o_delta, o_fox, proj, x2d, w_branch, w_out, final_norm_w if last else norm_w[l + 1],
                      sg_ln_w[l], sg_ln_b[l], w_spatial[l], b_spatial[l], l, last, tm_merge)
        if last:
            return outs[0].reshape(batch, seq, D_MODEL)
        x2d, h = outs
```

```python
import jax
import jax.numpy as jnp
from jax import lax
from jax.experimental import pallas as pl
from jax.experimental.pallas import tpu as pltpu

F32 = jnp.float32
BF16 = jnp.bfloat16

D_MODEL = 1024
HEAD_DIM = 128
N_HEADS = 4
WIDTH = N_HEADS * HEAD_DIM
N_BRANCHES = 3
DN_CHUNK = 64
SUPER = 2 * DN_CHUNK
CONV_WIDTH = 4
SG_CHUNK = 128
NORM_EPS = 1e-6
LN_EPS = 1e-5
L2_EPS = 1e-6
LANES = 128
SUBLANES = 8

GATES_OFF = 0
DN_OFF = GATES_OFF + N_BRANCHES * D_MODEL
SG_OFF = DN_OFF + 4 * WIDTH
FOX_OFF = SG_OFF + 3 * WIDTH
PACKED_COLS = FOX_OFF + 4 * WIDTH
PROJ_TILE_N = PACKED_COLS // 4
assert PROJ_TILE_N % LANES == 0
BETA_LANE, ALPHA_LANE, FORGET_LANE = 0, N_HEADS, 2 * N_HEADS

VMEM_LIMIT = 48 * 1024 * 1024


def _dot(a, b):
    return jnp.dot(a.astype(BF16), b.astype(BF16), preferred_element_type=F32)


def _dot_nt(a, b):
    return lax.dot_general(a.astype(BF16), b.astype(BF16), (((1,), (1,)), ((), ())),
                           preferred_element_type=F32)


def _dot_exact_lhs(a01, x):
    a = a01.astype(BF16)
    x1 = x.astype(BF16)
    r1 = x - x1.astype(F32)
    x2 = r1.astype(BF16)
    x3 = (r1 - x2.astype(F32)).astype(BF16)
    d = lambda p: jnp.dot(a, p, preferred_element_type=F32)
    return d(x1) + d(x2) + d(x3)


def _sigmoid(x):
    return 1.0 / (1.0 + jnp.exp(-x))


def _silu(x):
    return x * _sigmoid(x)


def _softplus(x):
    return jnp.maximum(x, 0.0) + jnp.log(1.0 + jnp.exp(-jnp.abs(x)))


def _log_sigmoid(x):
    return -_softplus(-x)


def _rms_norm(x, w):
    return x * lax.rsqrt(jnp.mean(x * x, axis=-1, keepdims=True) + NORM_EPS) * w


def _inproj_body(h_ref, w_ref, ws_ref, o_ref, small_ref):
    @pl.when(pl.program_id(1) == 0)
    def _():
        small_ref[...] = jnp.dot(h_ref[...], ws_ref[...], preferred_element_type=F32)

    o_ref[...] = jnp.dot(h_ref[...], w_ref[...], preferred_element_type=F32).astype(o_ref.dtype)


def _inproj(h, w_packed, w_small, layer, tm):
    n = h.shape[0]
    return pl.pallas_call(
        _inproj_body,
        grid=(n // tm, PACKED_COLS // PROJ_TILE_N),
        in_specs=[
            pl.BlockSpec((tm, D_MODEL), lambda i, j: (i, 0)),
            pl.BlockSpec((None, D_MODEL, PROJ_TILE_N), lambda i, j: (layer, 0, j)),
            pl.BlockSpec((None, D_MODEL, LANES), lambda i, j: (layer, 0, 0)),
        ],
        out_specs=[pl.BlockSpec((tm, PROJ_TILE_N), lambda i, j: (i, j)),
                   pl.BlockSpec((tm, LANES), lambda i, j: (i, 0))],
        out_shape=[jax.ShapeDtypeStruct((n, PACKED_COLS), BF16),
                   jax.ShapeDtypeStruct((n, LANES), F32)],
        compiler_params=pltpu.CompilerParams(
            dimension_semantics=("parallel", "arbitrary"), vmem_limit_bytes=VMEM_LIMIT),
        name="inproj",
    )(h, w_packed, w_small)


def _unit_lower_inverse(lows, eye, m8, moffs):
    each = lambda f, *lists: [f(*args) for args in zip(*lists)]
    l8 = each(lambda low: jnp.where(m8, low, 0.0), lows)
    l8_2 = each(_dot, l8, l8)
    l8_3 = each(_dot, l8, l8_2)
    l8_4 = each(_dot, l8_2, l8_2)
    p1 = each(lambda a, b, c: eye - a + b - c, l8, l8_2, l8_3)
    x = each(lambda p, l4: p + _dot(p, l4), p1, l8_4)
    for moff in moffs:
        xl = each(lambda xi, low: _dot(xi, jnp.where(moff, low, 0.0)), x, lows)
        x = each(lambda xi, xli: xi - _dot(xli, xi), x, xl)
    return x


def _delta_body(qkv_ref, z_ref, small_ref, convw_ref, alog_ref, dtb_ref, nw_ref, o_ref,
                halo_ref, y_ref, bt_ref, gt_ref, s_ref, u_ref, w_ref, qd_ref, qk_ref, kdt_ref, gl_ref):
    t = pl.program_id(0)
    nb, tm = qkv_ref.shape[0], qkv_ref.shape[1]
    assert tm & (tm - 1) == 0
    nsc = tm // SUPER
    halo = SUBLANES
    taps = CONV_WIDTH - 1

    @pl.when(t == 0)
    def _():
        halo_ref[...] = jnp.zeros(halo_ref.shape, F32)
        s_ref[...] = jnp.zeros(s_ref.shape, F32)

    for b in range(nb):
        xf = qkv_ref[b].astype(F32)
        halo_ref[b, halo:2 * halo, :] = xf[0:halo, :]
        acc = convw_ref[taps:taps + 1, :] * xf
        head = convw_ref[taps:taps + 1, :] * xf[0:halo, :]
        for s in range(1, taps + 1):
            wrow = convw_ref[taps - s:taps - s + 1, :]
            acc = acc + wrow * pltpu.roll(xf, s, axis=0)
            head = head + wrow * halo_ref[b, halo - s:2 * halo - s, :]
        y_ref[b] = _silu(acc)
        y_ref[b, 0:halo, :] = _silu(head)
        halo_ref[b, 0:halo, :] = xf[tm - halo:tm, :]
        sm = small_ref[b]
        bt_ref[b] = _sigmoid(sm)
        gt_ref[b] = -jnp.exp(alog_ref[...]) * _softplus(sm + dtb_ref[...])

    row = lax.broadcasted_iota(jnp.int32, (SUPER, SUPER), 0)
    col = lax.broadcasted_iota(jnp.int32, (SUPER, SUPER), 1)
    blk = lambda idx, size: jnp.right_shift(idx, size.bit_length() - 1)
    same_chunk = blk(row, DN_CHUNK) == blk(col, DN_CHUNK)
    causal = same_chunk & (col <= row)
    strict = same_chunk & (col < row)
    causal_ones = jnp.where(causal, 1.0, 0.0).astype(BF16)
    eye = jnp.where(row == col, 1.0, 0.0).astype(F32)
    m8 = blk(row, 8) == blk(col, 8)
    moffs = [(blk(row, 2 * s) == blk(col, 2 * s)) & (blk(row, s) != blk(col, s)) for s in (8, 16, 32)]
    first_rows = row < DN_CHUNK
    first_cols = col < DN_CHUNK
    nw = nw_ref[...]

    each = lambda f, *lists: [f(*args) for args in zip(*lists)]
    hcols = lambda h: slice(h * HEAD_DIM, (h + 1) * HEAD_DIM)
    srows = lambda sc: slice(sc * SUPER, (sc + 1) * SUPER)
    unit = lambda b, sc, h: (b * nsc + sc) * N_HEADS + h
    units = [(b, sc, h) for b in range(nb) for sc in range(nsc) for h in range(N_HEADS)]
    kbs, ks, qs, decs, egcs = [], [], [], [], []
    for b in range(nb):
        for sc in range(nsc):
            rows = srows(sc)
            gc = _dot_exact_lhs(causal_ones, gt_ref[b, rows, :])
            gct = gc.T
            bt = bt_ref[b, rows, :]
            egc = jnp.exp(gc)
            gl0 = gc[DN_CHUNK - 1:DN_CHUNK, :]
            gl1 = gc[SUPER - 1:SUPER, :]
            ekd = jnp.exp(jnp.where(first_rows, gl0, gl1) - gc)
            egl = (jnp.exp(gl0), jnp.exp(gl1))
            for h in range(N_HEADS):
                lane = slice(ALPHA_LANE + h, ALPHA_LANE + h + 1)
                q = y_ref[b, rows, h * HEAD_DIM:(h + 1) * HEAD_DIM]
                k = y_ref[b, rows, WIDTH + h * HEAD_DIM:WIDTH + (h + 1) * HEAD_DIM]
                q = q * lax.rsqrt(jnp.sum(q * q, axis=-1, keepdims=True) + L2_EPS) * (HEAD_DIM ** -0.5)
                k = k * lax.rsqrt(jnp.sum(k * k, axis=-1, keepdims=True) + L2_EPS)
                dec = jnp.exp(jnp.where(causal, gc[:, lane] - gct[lane, :], -jnp.inf))
                kb = k * bt[:, BETA_LANE + h:BETA_LANE + h + 1]
                qd_ref[b, rows, hcols(h)] = (q * egc[:, lane]).astype(BF16)
                kdt_ref[unit(b, sc, h)] = (k * ekd[:, lane]).T.astype(BF16)
                for c in range(2):
                    gl_ref[unit(b, sc, h) * 2 + c] = jnp.broadcast_to(egl[c][:, lane], (SUBLANES, LANES))
                qs.append(q), ks.append(k), kbs.append(kb), decs.append(dec), egcs.append(egc[:, lane])
    lows = each(lambda kb, k, dec: jnp.where(strict, _dot_nt(kb, k) * dec, 0.0), kbs, ks, decs)
    qks = each(lambda q, k, dec: (_dot_nt(q, k) * dec).astype(BF16), qs, ks, decs)
    tmats = _unit_lower_inverse(lows, eye, m8, moffs)
    for (b, sc, h), tmat, kb, egc_col, qk in zip(units, tmats, kbs, egcs, qks):
        rows = srows(sc)
        v = y_ref[b, rows, 2 * WIDTH + h * HEAD_DIM:2 * WIDTH + (h + 1) * HEAD_DIM]
        u_ref[b, rows, hcols(h)] = _dot(tmat, v * bt_ref[b, rows, BETA_LANE + h:BETA_LANE + h + 1])
        w_ref[b, rows, hcols(h)] = _dot(tmat, kb * egc_col).astype(BF16)
        qk_ref[unit(b, sc, h)] = qk

    chains = [(b, h) for b in range(nb) for h in range(N_HEADS)]
    zero = jnp.zeros((HEAD_DIM, SUPER), BF16)
    for sc in range(nsc):
        rows = srows(sc)
        u = [u_ref[b, rows, hcols(h)] for b, h in chains]
        w = [w_ref[b, rows, hcols(h)] for b, h in chains]
        kdt = [kdt_ref[unit(b, sc, h)] for b, h in chains]
        kdt0 = [jnp.where(first_cols, x, zero) for x in kdt]
        kdt1 = [jnp.where(first_cols, zero, x) for x in kdt]
        g0 = [jnp.tile(gl_ref[unit(b, sc, h) * 2], (HEAD_DIM // SUBLANES, 1)) for b, h in chains]
        g1 = [jnp.tile(gl_ref[unit(b, sc, h) * 2 + 1], (HEAD_DIM // SUBLANES, 1)) for b, h in chains]
        s0 = [s_ref[b * N_HEADS + h] for b, h in chains]
        vn0 = each(lambda ui, wi, si: ui - _dot(wi, si), u, w, s0)
        s1 = each(lambda si, gi, ki, vi: si * gi + _dot(ki, vi), s0, g0, kdt0, vn0)
        vn1 = each(lambda ui, wi, si: ui - _dot(wi, si), u, w, s1)
        s2 = each(lambda si, gi, ki, vi: si * gi + _dot(ki, vi), s1, g1, kdt1, vn1)
        for ci, (b, h) in enumerate(chains):
            s_ref[b * N_HEADS + h] = s2[ci]
        qd = [qd_ref[b, rows, hcols(h)] for b, h in chains]
        o0 = each(_dot, qd, s0)
        o1 = each(_dot, qd, s1)
        vn = each(lambda a, c: jnp.where(first_rows, a, c), vn0, vn1)
        oq = each(lambda qki, vi: _dot(qki, vi), [qk_ref[unit(b, sc, h)] for b, h in chains], vn)
        for ci, (b, h) in enumerate(chains):
            o = jnp.where(first_rows, o0[ci], o1[ci]) + oq[ci]
            o = o * lax.rsqrt(jnp.mean(o * o, axis=-1, keepdims=True) + NORM_EPS) * nw
            o_ref[b, rows, hcols(h)] = (o * _silu(z_ref[b, rows, hcols(h)].astype(F32))).astype(o_ref.dtype)


def _delta_branch(proj, small, conv_w, a_log, dt_bias, norm_w, batch, seq, tm):
    n = proj.shape[0]
    nt = seq // tm
    units = batch * (tm // SUPER) * N_HEADS
    lane_row = lambda vals, off: jnp.zeros((1, LANES), F32).at[0, off:off + N_HEADS].set(vals)
    tok = lambda width, col: pl.BlockSpec((batch, tm, width), lambda t: (0, t, col))
    full = lambda shape: pl.BlockSpec(shape, lambda t: (0,) * len(shape))
    proj3 = proj.reshape(batch, seq, proj.shape[1])
    out = pl.pallas_call(
        _delta_body,
        grid=(nt,),
        in_specs=[
            tok(3 * WIDTH, DN_OFF // (3 * WIDTH)),
            tok(WIDTH, (DN_OFF + 3 * WIDTH) // WIDTH),
            tok(LANES, 0),
            full((CONV_WIDTH, 3 * WIDTH)),
            full((1, LANES)),
            full((1, LANES)),
            full((1, HEAD_DIM)),
        ],
        out_specs=tok(WIDTH, 0),
        out_shape=jax.ShapeDtypeStruct((batch, seq, WIDTH), BF16),
        scratch_shapes=[
            pltpu.VMEM((batch, 2 * SUBLANES, 3 * WIDTH), F32),
            pltpu.VMEM((batch, tm, 3 * WIDTH), F32),
            pltpu.VMEM((batch, tm, LANES), F32),
            pltpu.VMEM((batch, tm, LANES), F32),
            pltpu.VMEM((batch * N_HEADS, HEAD_DIM, HEAD_DIM), F32),
            pltpu.VMEM((batch, tm, WIDTH), F32),
            pltpu.VMEM((batch, tm, WIDTH), BF16),
            pltpu.VMEM((batch, tm, WIDTH), BF16),
            pltpu.VMEM((units, SUPER, SUPER), BF16),
            pltpu.VMEM((units, HEAD_DIM, SUPER), BF16),
            pltpu.VMEM((2 * units, SUBLANES, LANES), F32),
        ],
        compiler_params=pltpu.CompilerParams(
            dimension_semantics=("arbitrary",), vmem_limit_bytes=VMEM_LIMIT),
        name="delta",
    )(proj3, proj3, small.reshape(batch, seq, LANES), conv_w,
      lane_row(a_log, ALPHA_LANE), lane_row(dt_bias, ALPHA_LANE), norm_w.reshape(1, HEAD_DIM))
    return out.reshape(n, WIDTH)


def _sg_tile(u_ref, v_ref, z_ref, lnw_ref, lnb_ref, ws_ref, bs_ref):
    tm = u_ref.shape[0]
    v = v_ref[...].astype(F32)
    mu = jnp.mean(v, axis=-1, keepdims=True)
    vc = v - mu
    var = jnp.mean(vc * vc, axis=-1, keepdims=True)
    vn = (vc * lax.rsqrt(var + LN_EPS) * lnw_ref[...] + lnb_ref[...]).astype(BF16)
    row = lax.broadcasted_iota(jnp.int32, (SG_CHUNK, SG_CHUNK), 0)
    col = lax.broadcasted_iota(jnp.int32, (SG_CHUNK, SG_CHUNK), 1)
    tril = col <= row
    gate = u_ref[...].astype(F32) * _silu(z_ref[...].astype(F32))
    bs = bs_ref[...]
    cols = []
    for g in range(N_HEADS):
        wg = jnp.where(tril, ws_ref[g], 0.0).astype(BF16)
        bg = bs[:, g:g + 1]
        cs = slice(g * HEAD_DIM, (g + 1) * HEAD_DIM)
        chunks = []
        for c in range(tm // SG_CHUNK):
            rs = slice(c * SG_CHUNK, (c + 1) * SG_CHUNK)
            mixed = jnp.dot(wg, vn[rs, cs], preferred_element_type=F32) + bg
            chunks.append((gate[rs, cs] * mixed).astype(BF16))
        cols.append(jnp.concatenate(chunks, axis=0))
    return jnp.concatenate(cols, axis=1)


LOG2E = 1.4426950408889634
ATTN_SUB = 256
ATTN_STAGES_PER_REGION = 6


def _fox_prep_body(q_ref, k_ref, v_ref, small_ref, fb_ref, qw_ref, kw_ref,
                   qn_ref, kn_ref, vt_ref, ccol_ref, crow_ref, carry_ref):
    t = pl.program_id(1)
    tm = q_ref.shape[0]

    @pl.when(t == 0)
    def _():
        carry_ref[...] = jnp.zeros(carry_ref.shape, F32)

    for h in range(N_HEADS):
        cs = slice(h * HEAD_DIM, (h + 1) * HEAD_DIM)
        q = q_ref[:, cs].astype(F32)
        k = k_ref[:, cs].astype(F32)
        q = q * lax.rsqrt(jnp.mean(q * q, axis=-1, keepdims=True) + NORM_EPS) * qw_ref[...]
        k = k * lax.rsqrt(jnp.mean(k * k, axis=-1, keepdims=True) + NORM_EPS) * kw_ref[...]
        qn_ref[:, cs] = (q * (HEAD_DIM ** -0.5 * LOG2E)).astype(BF16)
        kn_ref[:, cs] = k.astype(BF16)
        vt_ref[0, h, 0] = v_ref[:, cs].astype(F32).T.astype(BF16)

    logf = _log_sigmoid(small_ref[...] + fb_ref[...])
    row = lax.broadcasted_iota(jnp.int32, (tm, tm), 0)
    col = lax.broadcasted_iota(jnp.int32, (tm, tm), 1)
    c = _dot_exact_lhs(jnp.where(col <= row, 1.0, 0.0), logf) + carry_ref[0:1, :]
    carry_ref[0:1, :] = c[tm - 1:tm, :]
    c2 = c * LOG2E
    ct = c2.T
    for h in range(N_HEADS):
        lane = FORGET_LANE + h
        ccol_ref[:, h * HEAD_DIM:(h + 1) * HEAD_DIM] = jnp.broadcast_to(c2[:, lane:lane + 1], (tm, HEAD_DIM))
        crow_ref[h * SUBLANES:(h + 1) * SUBLANES, :] = jnp.broadcast_to(ct[lane:lane + 1, :], (SUBLANES, tm))


def _fox_prep(proj, small, f_bias, qn_w, kn_w, batch, seq, tm):
    n = proj.shape[0]
    nt = seq // tm
    tok = lambda width, col: pl.BlockSpec((tm, width), lambda b, t: (b * nt + t, col))
    full = lambda shape: pl.BlockSpec(shape, lambda b, t: (0,) * len(shape))
    fb = jnp.zeros((1, LANES), F32).at[0, FORGET_LANE:FORGET_LANE + N_HEADS].set(f_bias)
    out_tok = pl.BlockSpec((tm, WIDTH), lambda b, t: (b * nt + t, 0))
    return pl.pallas_call(
        _fox_prep_body,
        grid=(batch, nt),
        in_specs=[
            tok(WIDTH, FOX_OFF // WIDTH), tok(WIDTH, FOX_OFF // WIDTH + 1), tok(WIDTH, FOX_OFF // WIDTH + 2),
            tok(LANES, 0),
            full((1, LANES)), full((1, HEAD_DIM)), full((1, HEAD_DIM)),
        ],
        out_specs=[out_tok, out_tok,
                   pl.BlockSpec((1, N_HEADS, 1, HEAD_DIM, tm), lambda b, t: (b, 0, t, 0, 0)),
                   out_tok,
                   pl.BlockSpec((N_HEADS * SUBLANES, tm), lambda b, t: (b, t))],
        out_shape=[jax.ShapeDtypeStruct((n, WIDTH), BF16)] * 2
        + [jax.ShapeDtypeStruct((batch, N_HEADS, nt, HEAD_DIM, tm), BF16),
           jax.ShapeDtypeStruct((n, WIDTH), F32),
           jax.ShapeDtypeStruct((batch * N_HEADS * SUBLANES, seq), F32)],
        scratch_shapes=[pltpu.VMEM((SUBLANES, LANES), F32)],
        compiler_params=pltpu.CompilerParams(
            dimension_semantics=("parallel", "arbitrary"), vmem_limit_bytes=VMEM_LIMIT),
        name="fox_prep",
    )(proj, proj, proj, small, fb, qn_w.reshape(1, HEAD_DIM), kn_w.reshape(1, HEAD_DIM))


def _fox_attn_body(go_ref, q_ref, k_ref, vt_ref, cq_ref, ck_ref, z_ref, o_ref,
                   m_ref, l_ref, acc_ref, sta_ref, stb_ref):
    blk = sta_ref.shape[1]
    nblk = q_ref.shape[0] // blk
    subs = [slice(s * ATTN_SUB, (s + 1) * ATTN_SUB) for s in range(blk // ATTN_SUB)]
    pairs = [(i, j) for i in range(nblk) for j in range(i + 1)]
    bufs = (sta_ref, stb_ref)

    def key_counts(i, j):
        return [s.stop if j == i else blk for s in subs]

    def scores(i, j, st_ref, si):
        s, nk = subs[si], key_counts(i, j)[si]
        rows = slice(j * blk, j * blk + nk)
        ck = jnp.tile(ck_ref[rows, :], (1, ATTN_SUB // HEAD_DIM))
        st_ref[si, 0:nk, :] = lax.dot_general(
            k_ref[rows, :], q_ref[i * blk + s.start:i * blk + s.stop, :], (((1,), (1,)), ((), ())),
            preferred_element_type=F32) - ck

    def softmax_pv(i, j, st_ref, si):
        s, nk = subs[si], key_counts(i, j)[si]
        st = st_ref[si, 0:nk, :]
        cq = cq_ref[0:1, i * blk + s.start:i * blk + s.stop]
        if j == i:
            kpos = lax.broadcasted_iota(jnp.int32, (nk, ATTN_SUB), 0)
            qpos = lax.broadcasted_iota(jnp.int32, (nk, ATTN_SUB), 1)
            st = jnp.where(kpos <= qpos + s.start, st, -jnp.inf)
        m_new = jnp.max(st, axis=0, keepdims=True) + cq
        if j > 0:
            m_prev = m_ref[0:1, s]
            m_new = jnp.maximum(m_prev, m_new)
            alpha = jnp.exp2(m_prev - m_new)
        p = jnp.exp2(st + (cq - m_new))
        lhs = jnp.concatenate([vt_ref[0, 0, j, :, 0:nk], jnp.ones((2 * SUBLANES, nk), BF16)], axis=0)
        pvs = jnp.dot(lhs, p.astype(BF16), preferred_element_type=F32)
        pv, psum = pvs[0:HEAD_DIM, :], pvs[HEAD_DIM:HEAD_DIM + 1, :]
        if j == 0:
            l_ref[0:1, s] = psum
            acc_ref[:, s] = pv
        else:
            l_ref[0:1, s] = alpha * l_ref[0:1, s] + psum
            acc_ref[:, s] = alpha * acc_ref[:, s] + pv
        m_ref[0:1, s] = m_new

    def finish(i):
        rows = slice(i * blk, (i + 1) * blk)
        o = (acc_ref[...] / l_ref[0:1, :]).T
        o_ref[rows, :] = (o * _silu(z_ref[rows, :].astype(F32))).astype(o_ref.dtype)

    def stage(n):
        for si in range(len(subs)):
            if n + 1 < len(pairs):
                scores(*pairs[n + 1], bufs[(n + 1) % 2], si)
            if n >= 0:
                softmax_pv(*pairs[n], bufs[n % 2], si)
        if n >= 0 and pairs[n][0] == pairs[n][1]:
            finish(pairs[n][0])

    def region(stages):
        @pl.when(go_ref[0] > 0)
        def _():
            for n in stages:
                stage(n)

    order = list(range(-1, len(pairs)))
    for r in range(0, len(order), ATTN_STAGES_PER_REGION):
        region(order[r:r + ATTN_STAGES_PER_REGION])


def _fox_attention(qn, kn, vt, ccol, crow, proj, batch, seq, blk):
    n = qn.shape[0]
    nb = seq // blk
    assert vt.shape == (batch, N_HEADS, nb, HEAD_DIM, blk)
    seq_spec = lambda arr_col: pl.BlockSpec((seq, HEAD_DIM), lambda b, h: (b, arr_col(h)))
    return pl.pallas_call(
        _fox_attn_body,
        grid=(batch, N_HEADS),
        in_specs=[
            pl.BlockSpec(memory_space=pltpu.SMEM),
            seq_spec(lambda h: h), seq_spec(lambda h: h),
            pl.BlockSpec((1, 1, nb, HEAD_DIM, blk), lambda b, h: (b, h, 0, 0, 0)),
            pl.BlockSpec((SUBLANES, seq), lambda b, h: (b * N_HEADS + h, 0)),
            seq_spec(lambda h: h),
            seq_spec(lambda h: (FOX_OFF + 3 * WIDTH) // HEAD_DIM + h),
        ],
        out_specs=seq_spec(lambda h: h),
        out_shape=jax.ShapeDtypeStruct((n, WIDTH), BF16),
        scratch_shapes=[pltpu.VMEM((SUBLANES, blk), F32), pltpu.VMEM((SUBLANES, blk), F32),
                        pltpu.VMEM((HEAD_DIM, blk), F32),
                        pltpu.VMEM((blk // ATTN_SUB, blk, ATTN_SUB), F32),
                        pltpu.VMEM((blk // ATTN_SUB, blk, ATTN_SUB), F32)],
        compiler_params=pltpu.CompilerParams(
            dimension_semantics=("parallel", "parallel"), vmem_limit_bytes=VMEM_LIMIT),
        name="fox_attention",
    )(jnp.ones((1,), jnp.int32), qn, kn, vt, crow, ccol, proj)


def _merge_body(od_ref, of_ref, g_ref, x_ref, wb_ref, wo_ref, nw_ref,
                u_ref, v_ref, z_ref, lnw_ref, lnb_ref, ws_ref, bs_ref, *out_refs):
    branches = (od_ref[...], _sg_tile(u_ref, v_ref, z_ref, lnw_ref, lnb_ref, ws_ref, bs_ref), of_ref[...])
    merged = None
    for nbr, branch in enumerate(branches):
        up = jnp.dot(branch, wb_ref[nbr], preferred_element_type=F32)
        term = _sigmoid(g_ref[:, nbr * D_MODEL:(nbr + 1) * D_MODEL].astype(F32)) * up
        merged = term if merged is None else merged + term
    x_new = x_ref[...] + jnp.dot(merged.astype(BF16), wo_ref[...], preferred_element_type=F32)
    normed_ref = out_refs[-1]
    normed_ref[...] = _rms_norm(x_new, nw_ref[...]).astype(normed_ref.dtype)
    if len(out_refs) == 2:
        out_refs[0][...] = x_new


def _merge(o_delta, o_fox, proj, x2d, w_branch, w_out, next_norm_w, sg_ln_w, sg_ln_b, w_s, b_s,
           layer, last, tm):
    n = x2d.shape[0]
    assert tm % SG_CHUNK == 0
    tok = lambda width, col: pl.BlockSpec((tm, width), lambda i: (i, col))
    full = lambda shape: pl.BlockSpec(shape, lambda i: (0,) * len(shape))
    bs_t = jnp.zeros((SG_CHUNK, LANES), F32).at[:, :N_HEADS].set(b_s.T)
    if last:
        out_specs = [tok(D_MODEL, 0)]
        out_shape = [jax.ShapeDtypeStruct((n, D_MODEL), F32)]
    else:
        out_specs = [tok(D_MODEL, 0), tok(D_MODEL, 0)]
        out_shape = [jax.ShapeDtypeStruct((n, D_MODEL), F32), jax.ShapeDtypeStruct((n, D_MODEL), BF16)]
    return pl.pallas_call(
        _merge_body,
        grid=(n // tm,),
        in_specs=[
            tok(WIDTH, 0), tok(WIDTH, 0),
            tok(N_BRANCHES * D_MODEL, GATES_OFF // (N_BRANCHES * D_MODEL)),
            tok(D_MODEL, 0),
            pl.BlockSpec((None, N_BRANCHES, WIDTH, D_MODEL), lambda i: (layer, 0, 0, 0)),
            pl.BlockSpec((None, D_MODEL, D_MODEL), lambda i: (layer, 0, 0)),
            full((1, D_MODEL)),
            tok(WIDTH, SG_OFF // WIDTH), tok(WIDTH, SG_OFF // WIDTH + 1), tok(WIDTH, SG_OFF // WIDTH + 2),
            full((1, WIDTH)), full((1, WIDTH)),
            full((N_HEADS, SG_CHUNK, SG_CHUNK)), full((SG_CHUNK, LANES)),
        ],
        out_specs=out_specs,
        out_shape=out_shape,
        compiler_params=pltpu.CompilerParams(
            dimension_semantics=("parallel",), vmem_limit_bytes=VMEM_LIMIT),
        name="merge",
    )(o_delta, o_fox, proj, x2d, w_branch, w_out, next_norm_w.reshape(1, D_MODEL),
      proj, proj, proj, sg_ln_w.reshape(1, WIDTH), sg_ln_b.reshape(1, WIDTH), w_s, bs_t)


def _first_norm_body(x_ref, w_ref, o_ref):
    o_ref[...] = _rms_norm(x_ref[...], w_ref[...]).astype(o_ref.dtype)


def _first_norm(x2d, w, tm):
    n = x2d.shape[0]
    return pl.pallas_call(
        _first_norm_body,
        grid=(n // tm,),
        in_specs=[pl.BlockSpec((tm, D_MODEL), lambda i: (i, 0)),
                  pl.BlockSpec((1, D_MODEL), lambda i: (0, 0))],
        out_specs=pl.BlockSpec((tm, D_MODEL), lambda i: (i, 0)),
        out_shape=jax.ShapeDtypeStruct((n, D_MODEL), BF16),
        compiler_params=pltpu.CompilerParams(dimension_semantics=("parallel",)),
        name="first_norm",
    )(x2d, w.reshape(1, D_MODEL))


_IN_SIZES = (3 * WIDTH, WIDTH, N_HEADS, N_HEADS, 2 * WIDTH, WIDTH, 3 * WIDTH, WIDTH, N_HEADS,
             N_BRANCHES * D_MODEL)
_IN_STARTS = tuple(sum(_IN_SIZES[:i]) for i in range(len(_IN_SIZES) + 1))
D_IN = _IN_STARTS[-1]
_PACK_RUNS = (
    (_IN_STARTS[9], _IN_SIZES[9], GATES_OFF),
    (_IN_STARTS[0], _IN_SIZES[0] + _IN_SIZES[1], DN_OFF),
    (_IN_STARTS[4], _IN_SIZES[4] + _IN_SIZES[5], SG_OFF),
    (_IN_STARTS[6], _IN_SIZES[6] + _IN_SIZES[7], FOX_OFF),
)
_SMALL_RUNS = (
    (_IN_STARTS[2], _IN_SIZES[2] + _IN_SIZES[3], BETA_LANE),
    (_IN_STARTS[8], _IN_SIZES[8], FORGET_LANE),
)


PACK_TILE = LANES


def _pack_body(w_ref, o_ref):
    for l in range(o_ref.shape[0]):
        o_ref[l] = w_ref[:, l, :].T.astype(BF16)


def _pack_src_start(j):
    col = j * PACK_TILE
    start = jnp.int32(0)
    for src, length, dst in _PACK_RUNS:
        start = jnp.where((col >= dst) & (col < dst + length), col - dst + src, start)
    return start


def _pack_w_in(w_in):
    depth, d_model, d_in = w_in.shape
    assert d_in == D_IN and all(length % PACK_TILE == 0 and dst % PACK_TILE == 0 for _, length, dst in _PACK_RUNS)
    packed = pl.pallas_call(
        _pack_body,
        grid=(PACKED_COLS // PACK_TILE,),
        in_specs=[pl.BlockSpec((pl.Element(PACK_TILE), pl.Element(depth), pl.Element(d_model)),
                               lambda j: (_pack_src_start(j), 0, 0))],
        out_specs=pl.BlockSpec((depth, d_model, PACK_TILE), lambda j: (0, 0, j)),
        out_shape=jax.ShapeDtypeStruct((depth, d_model, PACKED_COLS), BF16),
        compiler_params=pltpu.CompilerParams(
            dimension_semantics=("parallel",), vmem_limit_bytes=VMEM_LIMIT),
        name="pack_w_in",
    )(jnp.transpose(w_in, (2, 0, 1)))
    small = jnp.zeros((depth, d_model, LANES), BF16)
    for src, length, dst in _SMALL_RUNS:
        small = small.at[:, :, dst:dst + length].set(w_in[:, :, src:src + length].astype(BF16))
    return packed, small


def _tile(total, want):
    t = min(total, want)
    assert total % t == 0, (total, t)
    return t


def kernel(x, norm_w, w_in, f_bias, conv_w, a_log, dt_bias, dn_norm_w, sg_ln_w, sg_ln_b,
           w_spatial, b_spatial, fox_qnorm_w, fox_knorm_w, w_branch, w_out, final_norm_w):
    batch, seq, d_model = x.shape
    assert d_model == D_MODEL and seq % SUPER == 0
    n = batch * seq
    depth = w_in.shape[0]
    x2d = x.reshape(n, D_MODEL)
    tm_proj = _tile(n, 2048)
    tm_delta = _tile(seq, SUPER)
    tm_prep = _tile(seq, 1024)
    blk_attn = _tile(seq, 1024)
    tm_merge = _tile(n, 512)
    w_packed, w_small = _pack_w_in(w_in)
    w_branch = w_branch.astype(BF16)
    w_out = w_out.astype(BF16)
    h = _first_norm(x2d, norm_w[0], _tile(n, 1024))
    for l in range(depth):
        last = l == depth - 1
        proj, small = _inproj(h, w_packed, w_small, l, tm_proj)
        o_delta = _delta_branch(proj, small, conv_w[l], a_log[l], dt_bias[l], dn_norm_w[l],
                                batch, seq, tm_delta)
        qn, kn, vt, ccol, crow = _fox_prep(proj, small, f_bias[l], fox_qnorm_w[l], fox_knorm_w[l],
                                           batch, seq, tm_prep)
        o_fox = _fox_attention(qn, kn, vt, ccol, crow, proj, batch, seq, blk_attn)
        outs = _merge(o_delta, o_fox, proj, x2d, w_branch, w_out, final_norm_w if last else norm_w[l + 1],
                      sg_ln_w[l], sg_ln_b[l], w_spatial[l], b_spatial[l], l, last, tm_merge)
        if last:
            return outs[0].reshape(batch, seq, D_MODEL)
        x2d, h = outs
```

```python
import jax
import jax.numpy as jnp
from jax import lax
from jax.experimental import pallas as pl
from jax.experimental.pallas import tpu as pltpu

F32 = jnp.float32
BF16 = jnp.bfloat16

D_MODEL = 1024
HEAD_DIM = 128
N_HEADS = 4
WIDTH = N_HEADS * HEAD_DIM
N_BRANCHES = 3
DN_CHUNK = 64
SUPER = 2 * DN_CHUNK
CONV_WIDTH = 4
SG_CHUNK = 128
MERGE_SUB = 256
NORM_EPS = 1e-6
LN_EPS = 1e-5
L2_EPS = 1e-6
LANES = 128
SUBLANES = 8

GATES_OFF = 0
DN_OFF = GATES_OFF + N_BRANCHES * D_MODEL
SG_OFF = DN_OFF + 4 * WIDTH
FOX_OFF = SG_OFF + 3 * WIDTH
PACKED_COLS = FOX_OFF + 4 * WIDTH
PROJ_TILE_N = PACKED_COLS // 4
assert PROJ_TILE_N % LANES == 0
BETA_LANE, ALPHA_LANE, FORGET_LANE = 0, N_HEADS, 2 * N_HEADS

VMEM_LIMIT = 48 * 1024 * 1024


def _dot(a, b):
    return jnp.dot(a.astype(BF16), b.astype(BF16), preferred_element_type=F32)


def _dot_nt(a, b):
    return lax.dot_general(a.astype(BF16), b.astype(BF16), (((1,), (1,)), ((), ())),
                           preferred_element_type=F32)


def _dot_exact_lhs(a01, x):
    a = a01.astype(BF16)
    x1 = x.astype(BF16)
    r1 = x - x1.astype(F32)
    x2 = r1.astype(BF16)
    x3 = (r1 - x2.astype(F32)).astype(BF16)
    d = lambda p: jnp.dot(a, p, preferred_element_type=F32)
    return d(x1) + d(x2) + d(x3)


def _sigmoid(x):
    return 0.5 * jnp.tanh(0.5 * x) + 0.5


def _silu(x):
    h = 0.5 * x
    return h * (jnp.tanh(h) + 1.0)


def _softplus(x):
    return jnp.maximum(x, 0.0) + jnp.log(1.0 + jnp.exp(-jnp.abs(x)))


def _log_sigmoid(x):
    return -_softplus(-x)


def _rms_norm(x, w):
    return x * lax.rsqrt(jnp.mean(x * x, axis=-1, keepdims=True) + NORM_EPS) * w


def _inproj_body(h_ref, w_ref, ws_ref, o_ref, small_ref):
    @pl.when(pl.program_id(1) == 0)
    def _():
        small_ref[...] = jnp.dot(h_ref[...], ws_ref[...], preferred_element_type=F32)

    o_ref[...] = jnp.dot(h_ref[...], w_ref[...], preferred_element_type=F32).astype(o_ref.dtype)


def _inproj(h, w_packed, w_small, layer, tm):
    n = h.shape[0]
    return pl.pallas_call(
        _inproj_body,
        grid=(n // tm, PACKED_COLS // PROJ_TILE_N),
        in_specs=[
            pl.BlockSpec((tm, D_MODEL), lambda i, j: (i, 0)),
            pl.BlockSpec((None, D_MODEL, PROJ_TILE_N), lambda i, j: (layer, 0, j)),
            pl.BlockSpec((None, D_MODEL, LANES), lambda i, j: (layer, 0, 0)),
        ],
        out_specs=[pl.BlockSpec((tm, PROJ_TILE_N), lambda i, j: (i, j)),
                   pl.BlockSpec((tm, LANES), lambda i, j: (i, 0))],
        out_shape=[jax.ShapeDtypeStruct((n, PACKED_COLS), BF16),
                   jax.ShapeDtypeStruct((n, LANES), F32)],
        compiler_params=pltpu.CompilerParams(
            dimension_semantics=("parallel", "arbitrary"), vmem_limit_bytes=VMEM_LIMIT),
        name="inproj",
    )(h, w_packed, w_small)


def _unit_lower_inverse(lows, eye, m8, moffs):
    each = lambda f, *lists: [f(*args) for args in zip(*lists)]
    l8 = each(lambda low: jnp.where(m8, low, 0.0), lows)
    l8_2 = each(_dot, l8, l8)
    l8_3 = each(_dot, l8, l8_2)
    l8_4 = each(_dot, l8_2, l8_2)
    p1 = each(lambda a, b, c: eye - a + b - c, l8, l8_2, l8_3)
    x = each(lambda p, l4: p + _dot(p, l4), p1, l8_4)
    for moff in moffs:
        xl = each(lambda xi, low: _dot(xi, jnp.where(moff, low, 0.0)), x, lows)
        x = each(lambda xi, xli: xi - _dot(xli, xi), x, xl)
    return x


def _delta_body(qkv_ref, z_ref, small_ref, convw_ref, alog_ref, dtb_ref, nw_ref, o_ref,
                halo_ref, y_ref, bt_ref, gt_ref, s_ref, u_ref, w_ref, qd_ref, qk_ref, kdt_ref, gl_ref):
    t = pl.program_id(0)
    nb, tm = qkv_ref.shape[0], qkv_ref.shape[1]
    assert tm & (tm - 1) == 0
    nsc = tm // SUPER
    halo = SUBLANES
    taps = CONV_WIDTH - 1

    @pl.when(t == 0)
    def _():
        halo_ref[...] = jnp.zeros(halo_ref.shape, F32)
        s_ref[...] = jnp.zeros(s_ref.shape, F32)

    for b in range(nb):
        xf = qkv_ref[b].astype(F32)
        halo_ref[b, halo:2 * halo, :] = xf[0:halo, :]
        acc = convw_ref[taps:taps + 1, :] * xf
        head = convw_ref[taps:taps + 1, :] * xf[0:halo, :]
        for s in range(1, taps + 1):
            wrow = convw_ref[taps - s:taps - s + 1, :]
            acc = acc + wrow * pltpu.roll(xf, s, axis=0)
            head = head + wrow * halo_ref[b, halo - s:2 * halo - s, :]
        y_ref[b] = _silu(acc)
        y_ref[b, 0:halo, :] = _silu(head)
        halo_ref[b, 0:halo, :] = xf[tm - halo:tm, :]
        sm = small_ref[b]
        bt_ref[b] = _sigmoid(sm)
        gt_ref[b] = -jnp.exp(alog_ref[...]) * _softplus(sm + dtb_ref[...])

    row = lax.broadcasted_iota(jnp.int32, (SUPER, SUPER), 0)
    col = lax.broadcasted_iota(jnp.int32, (SUPER, SUPER), 1)
    blk = lambda idx, size: jnp.right_shift(idx, size.bit_length() - 1)
    same_chunk = blk(row, DN_CHUNK) == blk(col, DN_CHUNK)
    causal = same_chunk & (col <= row)
    strict = same_chunk & (col < row)
    causal_ones = jnp.where(causal, 1.0, 0.0).astype(BF16)
    eye = jnp.where(row == col, 1.0, 0.0).astype(F32)
    m8 = blk(row, 8) == blk(col, 8)
    moffs = [(blk(row, 2 * s) == blk(col, 2 * s)) & (blk(row, s) != blk(col, s)) for s in (8, 16, 32)]
    first_rows = row < DN_CHUNK
    first_cols = col < DN_CHUNK
    nw = nw_ref[...]

    each = lambda f, *lists: [f(*args) for args in zip(*lists)]
    hcols = lambda h: slice(h * HEAD_DIM, (h + 1) * HEAD_DIM)
    srows = lambda sc: slice(sc * SUPER, (sc + 1) * SUPER)
    unit = lambda b, sc, h: (b * nsc + sc) * N_HEADS + h
    units = [(b, sc, h) for b in range(nb) for sc in range(nsc) for h in range(N_HEADS)]
    kbs, ks, qs, decs, egcs = [], [], [], [], []
    for b in range(nb):
        for sc in range(nsc):
            rows = srows(sc)
            gc = _dot_exact_lhs(causal_ones, gt_ref[b, rows, :])
            gct = gc.T
            bt = bt_ref[b, rows, :]
            egc = jnp.exp(gc)
            gl0 = gc[DN_CHUNK - 1:DN_CHUNK, :]
            gl1 = gc[SUPER - 1:SUPER, :]
            ekd = jnp.exp(jnp.where(first_rows, gl0, gl1) - gc)
            egl = (jnp.exp(gl0), jnp.exp(gl1))
            for h in range(N_HEADS):
                lane = slice(ALPHA_LANE + h, ALPHA_LANE + h + 1)
                q = y_ref[b, rows, h * HEAD_DIM:(h + 1) * HEAD_DIM]
                k = y_ref[b, rows, WIDTH + h * HEAD_DIM:WIDTH + (h + 1) * HEAD_DIM]
                q = q * lax.rsqrt(jnp.sum(q * q, axis=-1, keepdims=True) + L2_EPS) * (HEAD_DIM ** -0.5)
                k = k * lax.rsqrt(jnp.sum(k * k, axis=-1, keepdims=True) + L2_EPS)
                dec = jnp.exp(jnp.where(causal, gc[:, lane] - gct[lane, :], -jnp.inf))
                kb = k * bt[:, BETA_LANE + h:BETA_LANE + h + 1]
                qd_ref[b, rows, hcols(h)] = (q * egc[:, lane]).astype(BF16)
                kdt_ref[unit(b, sc, h)] = (k * ekd[:, lane]).T.astype(BF16)
                for c in range(2):
                    gl_ref[unit(b, sc, h) * 2 + c] = jnp.broadcast_to(egl[c][:, lane], (SUBLANES, LANES))
                qs.append(q), ks.append(k), kbs.append(kb), decs.append(dec), egcs.append(egc[:, lane])
    lows = each(lambda kb, k, dec: jnp.where(strict, _dot_nt(kb, k) * dec, 0.0), kbs, ks, decs)
    qks = each(lambda q, k, dec: (_dot_nt(q, k) * dec).astype(BF16), qs, ks, decs)
    tmats = _unit_lower_inverse(lows, eye, m8, moffs)
    for (b, sc, h), tmat, kb, egc_col, qk in zip(units, tmats, kbs, egcs, qks):
        rows = srows(sc)
        v = y_ref[b, rows, 2 * WIDTH + h * HEAD_DIM:2 * WIDTH + (h + 1) * HEAD_DIM]
        u_ref[b, rows, hcols(h)] = _dot(tmat, v * bt_ref[b, rows, BETA_LANE + h:BETA_LANE + h + 1])
        w_ref[b, rows, hcols(h)] = _dot(tmat, kb * egc_col).astype(BF16)
        qk_ref[unit(b, sc, h)] = qk

    chains = [(b, h) for b in range(nb) for h in range(N_HEADS)]
    zero = jnp.zeros((HEAD_DIM, SUPER), BF16)
    for sc in range(nsc):
        rows = srows(sc)
        u = [u_ref[b, rows, hcols(h)] for b, h in chains]
        w = [w_ref[b, rows, hcols(h)] for b, h in chains]
        kdt = [kdt_ref[unit(b, sc, h)] for b, h in chains]
        kdt0 = [jnp.where(first_cols, x, zero) for x in kdt]
        kdt1 = [jnp.where(first_cols, zero, x) for x in kdt]
        g0 = [jnp.tile(gl_ref[unit(b, sc, h) * 2], (HEAD_DIM // SUBLANES, 1)) for b, h in chains]
        g1 = [jnp.tile(gl_ref[unit(b, sc, h) * 2 + 1], (HEAD_DIM // SUBLANES, 1)) for b, h in chains]
        s0 = [s_ref[b * N_HEADS + h] for b, h in chains]
        vn0 = each(lambda ui, wi, si: ui - _dot(wi, si), u, w, s0)
        s1 = each(lambda si, gi, ki, vi: si * gi + _dot(ki, vi), s0, g0, kdt0, vn0)
        vn1 = each(lambda ui, wi, si: ui - _dot(wi, si), u, w, s1)
        s2 = each(lambda si, gi, ki, vi: si * gi + _dot(ki, vi), s1, g1, kdt1, vn1)
        for ci, (b, h) in enumerate(chains):
            s_ref[b * N_HEADS + h] = s2[ci]
        qd = [qd_ref[b, rows, hcols(h)] for b, h in chains]
        o0 = each(_dot, qd, s0)
        o1 = each(_dot, qd, s1)
        vn = each(lambda a, c: jnp.where(first_rows, a, c), vn0, vn1)
        oq = each(lambda qki, vi: _dot(qki, vi), [qk_ref[unit(b, sc, h)] for b, h in chains], vn)
        for ci, (b, h) in enumerate(chains):
            o = jnp.where(first_rows, o0[ci], o1[ci]) + oq[ci]
            o = o * lax.rsqrt(jnp.mean(o * o, axis=-1, keepdims=True) + NORM_EPS) * nw
            o_ref[b, rows, hcols(h)] = (o * _silu(z_ref[b, rows, hcols(h)].astype(F32))).astype(o_ref.dtype)


def _delta_branch(proj, small, conv_w, a_log, dt_bias, norm_w, batch, seq, tm):
    n = proj.shape[0]
    nt = seq // tm
    units = batch * (tm // SUPER) * N_HEADS
    lane_row = lambda vals, off: jnp.zeros((1, LANES), F32).at[0, off:off + N_HEADS].set(vals)
    tok = lambda width, col: pl.BlockSpec((batch, tm, width), lambda t: (0, t, col))
    full = lambda shape: pl.BlockSpec(shape, lambda t: (0,) * len(shape))
    proj3 = proj.reshape(batch, seq, proj.shape[1])
    out = pl.pallas_call(
        _delta_body,
        grid=(nt,),
        in_specs=[
            tok(3 * WIDTH, DN_OFF // (3 * WIDTH)),
            tok(WIDTH, (DN_OFF + 3 * WIDTH) // WIDTH),
            tok(LANES, 0),
            full((CONV_WIDTH, 3 * WIDTH)),
            full((1, LANES)),
            full((1, LANES)),
            full((1, HEAD_DIM)),
        ],
        out_specs=tok(WIDTH, 0),
        out_shape=jax.ShapeDtypeStruct((batch, seq, WIDTH), BF16),
        scratch_shapes=[
            pltpu.VMEM((batch, 2 * SUBLANES, 3 * WIDTH), F32),
            pltpu.VMEM((batch, tm, 3 * WIDTH), F32),
            pltpu.VMEM((batch, tm, LANES), F32),
            pltpu.VMEM((batch, tm, LANES), F32),
            pltpu.VMEM((batch * N_HEADS, HEAD_DIM, HEAD_DIM), F32),
            pltpu.VMEM((batch, tm, WIDTH), F32),
            pltpu.VMEM((batch, tm, WIDTH), BF16),
            pltpu.VMEM((batch, tm, WIDTH), BF16),
            pltpu.VMEM((units, SUPER, SUPER), BF16),
            pltpu.VMEM((units, HEAD_DIM, SUPER), BF16),
            pltpu.VMEM((2 * units, SUBLANES, LANES), F32),
        ],
        compiler_params=pltpu.CompilerParams(
            dimension_semantics=("arbitrary",), vmem_limit_bytes=VMEM_LIMIT),
        name="delta",
    )(proj3, proj3, small.reshape(batch, seq, LANES), conv_w,
      lane_row(a_log, ALPHA_LANE), lane_row(dt_bias, ALPHA_LANE), norm_w.reshape(1, HEAD_DIM))
    return out.reshape(n, WIDTH)


def _sg_tile(u_ref, v_ref, z_ref, lnw_ref, lnb_ref, ws_ref, bs_ref):
    tm = u_ref.shape[0]
    v = v_ref[...].astype(F32)
    mu = jnp.mean(v, axis=-1, keepdims=True)
    vc = v - mu
    var = jnp.mean(vc * vc, axis=-1, keepdims=True)
    vn = (vc * lax.rsqrt(var + LN_EPS) * lnw_ref[...] + lnb_ref[...]).astype(BF16)
    row = lax.broadcasted_iota(jnp.int32, (SG_CHUNK, SG_CHUNK), 0)
    col = lax.broadcasted_iota(jnp.int32, (SG_CHUNK, SG_CHUNK), 1)
    tril = col <= row
    gate = u_ref[...].astype(F32) * _silu(z_ref[...].astype(F32))
    bs = bs_ref[...]
    cols = []
    for g in range(N_HEADS):
        wg = jnp.where(tril, ws_ref[g], 0.0).astype(BF16)
        bg = bs[:, g:g + 1]
        cs = slice(g * HEAD_DIM, (g + 1) * HEAD_DIM)
        chunks = []
        for c in range(tm // SG_CHUNK):
            rs = slice(c * SG_CHUNK, (c + 1) * SG_CHUNK)
            mixed = jnp.dot(wg, vn[rs, cs], preferred_element_type=F32) + bg
            chunks.append((gate[rs, cs] * mixed).astype(BF16))
        cols.append(jnp.concatenate(chunks, axis=0))
    return jnp.concatenate(cols, axis=1)


LOG2E = 1.4426950408889634
ATTN_SUB = 256
ATTN_STAGES_PER_REGION = 6


def _fox_prep_body(q_ref, k_ref, v_ref, small_ref, fb_ref, qw_ref, kw_ref,
                   qn_ref, kn_ref, vt_ref, ccol_ref, crow_ref, carry_ref):
    t = pl.program_id(1)
    tm = q_ref.shape[0]

    @pl.when(t == 0)
    def _():
        carry_ref[...] = jnp.zeros(carry_ref.shape, F32)

    for h in range(N_HEADS):
        cs = slice(h * HEAD_DIM, (h + 1) * HEAD_DIM)
        q = q_ref[:, cs].astype(F32)
        k = k_ref[:, cs].astype(F32)
        q = q * lax.rsqrt(jnp.mean(q * q, axis=-1, keepdims=True) + NORM_EPS) * qw_ref[...]
        k = k * lax.rsqrt(jnp.mean(k * k, axis=-1, keepdims=True) + NORM_EPS) * kw_ref[...]
        qn_ref[:, cs] = (q * (HEAD_DIM ** -0.5 * LOG2E)).astype(BF16)
        kn_ref[:, cs] = k.astype(BF16)
        vt_ref[0, h, 0] = v_ref[:, cs].astype(F32).T.astype(BF16)

    logf = _log_sigmoid(small_ref[...] + fb_ref[...])
    row = lax.broadcasted_iota(jnp.int32, (tm, tm), 0)
    col = lax.broadcasted_iota(jnp.int32, (tm, tm), 1)
    c = _dot_exact_lhs(jnp.where(col <= row, 1.0, 0.0), logf) + carry_ref[0:1, :]
    carry_ref[0:1, :] = c[tm - 1:tm, :]
    c2 = c * LOG2E
    ct = c2.T
    for h in range(N_HEADS):
        lane = FORGET_LANE + h
        ccol_ref[:, h * HEAD_DIM:(h + 1) * HEAD_DIM] = jnp.broadcast_to(c2[:, lane:lane + 1], (tm, HEAD_DIM))
        crow_ref[h * SUBLANES:(h + 1) * SUBLANES, :] = jnp.broadcast_to(ct[lane:lane + 1, :], (SUBLANES, tm))


def _fox_prep(proj, small, f_bias, qn_w, kn_w, batch, seq, tm):
    n = proj.shape[0]
    nt = seq // tm
    tok = lambda width, col: pl.BlockSpec((tm, width), lambda b, t: (b * nt + t, col))
    full = lambda shape: pl.BlockSpec(shape, lambda b, t: (0,) * len(shape))
    fb = jnp.zeros((1, LANES), F32).at[0, FORGET_LANE:FORGET_LANE + N_HEADS].set(f_bias)
    out_tok = pl.BlockSpec((tm, WIDTH), lambda b, t: (b * nt + t, 0))
    return pl.pallas_call(
        _fox_prep_body,
        grid=(batch, nt),
        in_specs=[
            tok(WIDTH, FOX_OFF // WIDTH), tok(WIDTH, FOX_OFF // WIDTH + 1), tok(WIDTH, FOX_OFF // WIDTH + 2),
            tok(LANES, 0),
            full((1, LANES)), full((1, HEAD_DIM)), full((1, HEAD_DIM)),
        ],
        out_specs=[out_tok, out_tok,
                   pl.BlockSpec((1, N_HEADS, 1, HEAD_DIM, tm), lambda b, t: (b, 0, t, 0, 0)),
                   out_tok,
                   pl.BlockSpec((N_HEADS * SUBLANES, tm), lambda b, t: (b, t))],
        out_shape=[jax.ShapeDtypeStruct((n, WIDTH), BF16)] * 2
        + [jax.ShapeDtypeStruct((batch, N_HEADS, nt, HEAD_DIM, tm), BF16),
           jax.ShapeDtypeStruct((n, WIDTH), F32),
           jax.ShapeDtypeStruct((batch * N_HEADS * SUBLANES, seq), F32)],
        scratch_shapes=[pltpu.VMEM((SUBLANES, LANES), F32)],
        compiler_params=pltpu.CompilerParams(
            dimension_semantics=("parallel", "arbitrary"), vmem_limit_bytes=VMEM_LIMIT),
        name="fox_prep",
    )(proj, proj, proj, small, fb, qn_w.reshape(1, HEAD_DIM), kn_w.reshape(1, HEAD_DIM))


def _fox_attn_body(go_ref, q_ref, k_ref, vt_ref, cq_ref, ck_ref, z_ref, o_ref,
                   m_ref, l_ref, acc_ref, sta_ref, stb_ref):
    blk = sta_ref.shape[1]
    nblk = q_ref.shape[0] // blk
    subs = [slice(s * ATTN_SUB, (s + 1) * ATTN_SUB) for s in range(blk // ATTN_SUB)]
    pairs = [(i, j) for i in range(nblk) for j in range(i + 1)]
    bufs = (sta_ref, stb_ref)

    def key_counts(i, j):
        return [s.stop if j == i else blk for s in subs]

    def scores(i, j, st_ref, si):
        s, nk = subs[si], key_counts(i, j)[si]
        rows = slice(j * blk, j * blk + nk)
        ck = jnp.tile(ck_ref[rows, :], (1, ATTN_SUB // HEAD_DIM))
        st_ref[si, 0:nk, :] = lax.dot_general(
            k_ref[rows, :], q_ref[i * blk + s.start:i * blk + s.stop, :], (((1,), (1,)), ((), ())),
            preferred_element_type=F32) - ck

    def softmax_pv(i, j, st_ref, si):
        s, nk = subs[si], key_counts(i, j)[si]
        st = st_ref[si, 0:nk, :]
        cq = cq_ref[0:1, i * blk + s.start:i * blk + s.stop]
        if j == i:
            kpos = lax.broadcasted_iota(jnp.int32, (nk, ATTN_SUB), 0)
            qpos = lax.broadcasted_iota(jnp.int32, (nk, ATTN_SUB), 1)
            st = jnp.where(kpos <= qpos + s.start, st, -jnp.inf)
        m_new = jnp.max(st, axis=0, keepdims=True) + cq
        if j > 0:
            m_prev = m_ref[0:1, s]
            m_new = jnp.maximum(m_prev, m_new)
            alpha = jnp.exp2(m_prev - m_new)
        p = jnp.exp2(st + (cq - m_new))
        lhs = jnp.concatenate([vt_ref[0, 0, j, :, 0:nk], jnp.ones((2 * SUBLANES, nk), BF16)], axis=0)
        pvs = jnp.dot(lhs, p.astype(BF16), preferred_element_type=F32)
        pv, psum = pvs[0:HEAD_DIM, :], pvs[HEAD_DIM:HEAD_DIM + 1, :]
        if j == 0:
            l_ref[0:1, s] = psum
            acc_ref[:, s] = pv
        else:
            l_ref[0:1, s] = alpha * l_ref[0:1, s] + psum
            acc_ref[:, s] = alpha * acc_ref[:, s] + pv
        m_ref[0:1, s] = m_new

    def finish(i):
        rows = slice(i * blk, (i + 1) * blk)
        o = (acc_ref[...] / l_ref[0:1, :]).T
        o_ref[rows, :] = (o * _silu(z_ref[rows, :].astype(F32))).astype(o_ref.dtype)

    def stage(n):
        for si in range(len(subs)):
            if n + 1 < len(pairs):
                scores(*pairs[n + 1], bufs[(n + 1) % 2], si)
            if n >= 0:
                softmax_pv(*pairs[n], bufs[n % 2], si)
        if n >= 0 and pairs[n][0] == pairs[n][1]:
            finish(pairs[n][0])

    def region(stages):
        @pl.when(go_ref[0] > 0)
        def _():
            for n in stages:
                stage(n)

    order = list(range(-1, len(pairs)))
    for r in range(0, len(order), ATTN_STAGES_PER_REGION):
        region(order[r:r + ATTN_STAGES_PER_REGION])


def _fox_attention(qn, kn, vt, ccol, crow, proj, batch, seq, blk):
    n = qn.shape[0]
    nb = seq // blk
    assert vt.shape == (batch, N_HEADS, nb, HEAD_DIM, blk)
    seq_spec = lambda arr_col: pl.BlockSpec((seq, HEAD_DIM), lambda b, h: (b, arr_col(h)))
    return pl.pallas_call(
        _fox_attn_body,
        grid=(batch, N_HEADS),
        in_specs=[
            pl.BlockSpec(memory_space=pltpu.SMEM),
            seq_spec(lambda h: h), seq_spec(lambda h: h),
            pl.BlockSpec((1, 1, nb, HEAD_DIM, blk), lambda b, h: (b, h, 0, 0, 0)),
            pl.BlockSpec((SUBLANES, seq), lambda b, h: (b * N_HEADS + h, 0)),
            seq_spec(lambda h: h),
            seq_spec(lambda h: (FOX_OFF + 3 * WIDTH) // HEAD_DIM + h),
        ],
        out_specs=seq_spec(lambda h: h),
        out_shape=jax.ShapeDtypeStruct((n, WIDTH), BF16),
        scratch_shapes=[pltpu.VMEM((SUBLANES, blk), F32), pltpu.VMEM((SUBLANES, blk), F32),
                        pltpu.VMEM((HEAD_DIM, blk), F32),
                        pltpu.VMEM((blk // ATTN_SUB, blk, ATTN_SUB), F32),
                        pltpu.VMEM((blk // ATTN_SUB, blk, ATTN_SUB), F32)],
        compiler_params=pltpu.CompilerParams(
            dimension_semantics=("parallel", "parallel"), vmem_limit_bytes=VMEM_LIMIT),
        name="fox_attention",
    )(jnp.ones((1,), jnp.int32), qn, kn, vt, crow, ccol, proj)


def _merge_body(od_ref, of_ref, g_ref, x_ref, wb_ref, wo_ref, nw_ref,
                u_ref, v_ref, z_ref, lnw_ref, lnb_ref, ws_ref, bs_ref, *out_refs):
    tm = x_ref.shape[0]
    subs = [slice(r, r + MERGE_SUB) for r in range(0, tm, MERGE_SUB)]
    sg = [_sg_tile(u_ref.at[r], v_ref.at[r], z_ref.at[r], lnw_ref, lnb_ref, ws_ref, bs_ref) for r in subs]
    branches = [(od_ref[r, :], sg_r, of_ref[r, :]) for r, sg_r in zip(subs, sg)]
    ups = [[jnp.dot(branch, wb_ref[nbr], preferred_element_type=F32) for nbr, branch in enumerate(brs)]
           for brs in branches]
    merged = []
    for r, up in zip(subs, ups):
        acc = None
        for nbr in range(N_BRANCHES):
            term = _sigmoid(g_ref[r, nbr * D_MODEL:(nbr + 1) * D_MODEL].astype(F32)) * up[nbr]
            acc = term if acc is None else acc + term
        merged.append(acc.astype(BF16))
    for r, m in zip(subs, merged):
        x_new = x_ref[r, :] + jnp.dot(m, wo_ref[...], preferred_element_type=F32)
        normed_ref = out_refs[-1]
        normed_ref[r, :] = _rms_norm(x_new, nw_ref[...]).astype(normed_ref.dtype)
        if len(out_refs) == 2:
            out_refs[0][r, :] = x_new


def _merge(o_delta, o_fox, proj, x2d, w_branch, w_out, next_norm_w, sg_ln_w, sg_ln_b, w_s, b_s,
           layer, last, tm):
    n = x2d.shape[0]
    assert tm % SG_CHUNK == 0
    tok = lambda width, col: pl.BlockSpec((tm, width), lambda i: (i, col))
    full = lambda shape: pl.BlockSpec(shape, lambda i: (0,) * len(shape))
    bs_t = jnp.zeros((SG_CHUNK, LANES), F32).at[:, :N_HEADS].set(b_s.T)
    if last:
        out_specs = [tok(D_MODEL, 0)]
        out_shape = [jax.ShapeDtypeStruct((n, D_MODEL), F32)]
    else:
        out_specs = [tok(D_MODEL, 0), tok(D_MODEL, 0)]
        out_shape = [jax.ShapeDtypeStruct((n, D_MODEL), F32), jax.ShapeDtypeStruct((n, D_MODEL), BF16)]
    return pl.pallas_call(
        _merge_body,
        grid=(n // tm,),
        in_specs=[
            tok(WIDTH, 0), tok(WIDTH, 0),
            tok(N_BRANCHES * D_MODEL, GATES_OFF // (N_BRANCHES * D_MODEL)),
            tok(D_MODEL, 0),
            pl.BlockSpec((None, N_BRANCHES, WIDTH, D_MODEL), lambda i: (layer, 0, 0, 0)),
            pl.BlockSpec((None, D_MODEL, D_MODEL), lambda i: (layer, 0, 0)),
            full((1, D_MODEL)),
            tok(WIDTH, SG_OFF // WIDTH), tok(WIDTH, SG_OFF // WIDTH + 1), tok(WIDTH, SG_OFF // WIDTH + 2),
            full((1, WIDTH)), full((1, WIDTH)),
            full((N_HEADS, SG_CHUNK, SG_CHUNK)), full((SG_CHUNK, LANES)),
        ],
        out_specs=out_specs,
        out_shape=out_shape,
        compiler_params=pltpu.CompilerParams(
            dimension_semantics=("parallel",), vmem_limit_bytes=VMEM_LIMIT),
        name="merge",
    )(o_delta, o_fox, proj, x2d, w_branch, w_out, next_norm_w.reshape(1, D_MODEL),
      proj, proj, proj, sg_ln_w.reshape(1, WIDTH), sg_ln_b.reshape(1, WIDTH), w_s, bs_t)


def _first_norm_body(x_ref, w_ref, o_ref):
    o_ref[...] = _rms_norm(x_ref[...], w_ref[...]).astype(o_ref.dtype)


def _first_norm(x2d, w, tm):
    n = x2d.shape[0]
    return pl.pallas_call(
        _first_norm_body,
        grid=(n // tm,),
        in_specs=[pl.BlockSpec((tm, D_MODEL), lambda i: (i, 0)),
                  pl.BlockSpec((1, D_MODEL), lambda i: (0, 0))],
        out_specs=pl.BlockSpec((tm, D_MODEL), lambda i: (i, 0)),
        out_shape=jax.ShapeDtypeStruct((n, D_MODEL), BF16),
        compiler_params=pltpu.CompilerParams(dimension_semantics=("parallel",)),
        name="first_norm",
    )(x2d, w.reshape(1, D_MODEL))


_IN_SIZES = (3 * WIDTH, WIDTH, N_HEADS, N_HEADS, 2 * WIDTH, WIDTH, 3 * WIDTH, WIDTH, N_HEADS,
             N_BRANCHES * D_MODEL)
_IN_STARTS = tuple(sum(_IN_SIZES[:i]) for i in range(len(_IN_SIZES) + 1))
D_IN = _IN_STARTS[-1]
_PACK_RUNS = (
    (_IN_STARTS[9], _IN_SIZES[9], GATES_OFF),
    (_IN_STARTS[0], _IN_SIZES[0] + _IN_SIZES[1], DN_OFF),
    (_IN_STARTS[4], _IN_SIZES[4] + _IN_SIZES[5], SG_OFF),
    (_IN_STARTS[6], _IN_SIZES[6] + _IN_SIZES[7], FOX_OFF),
)
_SMALL_RUNS = (
    (_IN_STARTS[2], _IN_SIZES[2] + _IN_SIZES[3], BETA_LANE),
    (_IN_STARTS[8], _IN_SIZES[8], FORGET_LANE),
)


PACK_TILE = LANES


def _pack_body(w_ref, o_ref):
    for l in range(o_ref.shape[0]):
        o_ref[l] = w_ref[:, l, :].T.astype(BF16)


def _pack_src_start(j):
    col = j * PACK_TILE
    start = jnp.int32(0)
    for src, length, dst in _PACK_RUNS:
        start = jnp.where((col >= dst) & (col < dst + length), col - dst + src, start)
    return start


def _pack_w_in(w_in):
    depth, d_model, d_in = w_in.shape
    assert d_in == D_IN and all(length % PACK_TILE == 0 and dst % PACK_TILE == 0 for _, length, dst in _PACK_RUNS)
    packed = pl.pallas_call(
        _pack_body,
        grid=(PACKED_COLS // PACK_TILE,),
        in_specs=[pl.BlockSpec((pl.Element(PACK_TILE), pl.Element(depth), pl.Element(d_model)),
                               lambda j: (_pack_src_start(j), 0, 0))],
        out_specs=pl.BlockSpec((depth, d_model, PACK_TILE), lambda j: (0, 0, j)),
        out_shape=jax.ShapeDtypeStruct((depth, d_model, PACKED_COLS), BF16),
        compiler_params=pltpu.CompilerParams(
            dimension_semantics=("parallel",), vmem_limit_bytes=VMEM_LIMIT),
        name="pack_w_in",
    )(jnp.transpose(w_in, (2, 0, 1)))
    small = jnp.zeros((depth, d_model, LANES), BF16)
    for src, length, dst in _SMALL_RUNS:
        small = small.at[:, :, dst:dst + length].set(w_in[:, :, src:src + length].astype(BF16))
    return packed, small


def _tile(total, want):
    t = min(total, want)
    assert total % t == 0, (total, t)
    return t


def kernel(x, norm_w, w_in, f_bias, conv_w, a_log, dt_bias, dn_norm_w, sg_ln_w, sg_ln_b,
           w_spatial, b_spatial, fox_qnorm_w, fox_knorm_w, w_branch, w_out, final_norm_w):
    batch, seq, d_model = x.shape
    assert d_model == D_MODEL and seq % SUPER == 0
    n = batch * seq
    depth = w_in.shape[0]
    x2d = x.reshape(n, D_MODEL)
    tm_proj = _tile(n, 2048)
    tm_delta = _tile(seq, SUPER)
    tm_prep = _tile(seq, 1024)
    blk_attn = _tile(seq, 1024)
    tm_merge = _tile(n, 512)
    w_packed, w_small = _pack_w_in(w_in)
    w_branch = w_branch.astype(BF16)
    w_out = w_out.astype(BF16)
    h = _first_norm(x2d, norm_w[0], _tile(n, 1024))
    for l in range(depth):
        last = l == depth - 1
        proj, small = _inproj(h, w_packed, w_small, l, tm_proj)
        o_delta = _delta_branch(proj, small, conv_w[l], a_log[l], dt_bias[l], dn_norm_w[l],
                                batch, seq, tm_delta)
        qn, kn, vt, ccol, crow = _fox_prep(proj, small, f_bias[l], fox_qnorm_w[l], fox_knorm_w[l],
                                           batch, seq, tm_prep)
        o_fox = _fox_attention(qn, kn, vt, ccol, crow, proj, batch, seq, blk_attn)
        outs = _merge(o_delta, o_fox, proj, x2d, w_branch, w_out, final_norm_w if last else norm_w[l + 1],
                      sg_ln_w[l], sg_ln_b[l], w_spatial[l], b_spatial[l], l, last, tm_merge)
        if last:
            return outs[0].reshape(batch, seq, D_MODEL)
        x2d, h = outs
```
